```python
import jax
import jax.numpy as jnp
from jax import lax
import numpy as np

D_MODEL = 1024
BATCH = 8
SEQ = 2048
DEPTH = 2

CTX_LEN = 256
GRID_W = 64
N_MIXERS = 2
N_REC_LAYERS = (DEPTH + 1) // 2
N_CONV_LAYERS = DEPTH // 2
D_RNN = (4 * D_MODEL // 3) // 128 * 128
N_RNN_BLOCKS = 16
RNN_BLOCK = D_RNN // N_RNN_BLOCKS
REC_CONV_W = 4
REC_CONV_PAD = (1, 2)
RG_C = 8.0
CONF_KW = 31
CONF_PAD = (CONF_KW // 2, CONF_KW // 2)
D_FF = 4 * D_MODEL
N_MOD = 6
EPS = 1e-6
POS_BASE = 10000.0

kernel_name = 'hybrid_rglru_conformer_dit_block'


def rmsnorm(x, g):
    xf = x.astype(jnp.float32)
    y = xf * lax.rsqrt(jnp.mean(xf * xf, axis=-1, keepdims=True) + EPS)
    return (y * g.astype(jnp.float32)).astype(x.dtype)


def layernorm(x, g, b):
    xf = x.astype(jnp.float32)
    mu = jnp.mean(xf, axis=-1, keepdims=True)
    var = jnp.mean(jnp.square(xf - mu), axis=-1, keepdims=True)
    y = (xf - mu) * lax.rsqrt(var + EPS)
    return (y * g.astype(jnp.float32) + b.astype(jnp.float32)).astype(x.dtype)


def modulate(h, shift, scale):
    return h * (1 + scale) + shift


def grid_pos_embed(rows, d, dtype):
    t = jnp.arange(rows * GRID_W, dtype=jnp.int32)
    row = (t // GRID_W).astype(jnp.float32)
    col = (t % GRID_W).astype(jnp.float32)
    q = d // 4
    omega = 1.0 / (POS_BASE ** (jnp.arange(q, dtype=jnp.float32) / q))
    er = row[:, None] * omega[None, :]
    ec = col[:, None] * omega[None, :]
    return jnp.concatenate([jnp.sin(er), jnp.cos(er), jnp.sin(ec), jnp.cos(ec)], axis=-1).astype(dtype)


def dwconv(x, w, b, pad):
    y = lax.conv_general_dilated(x, w[:, None, :].astype(x.dtype), window_strides=(1,), padding=[pad],
                                 dimension_numbers=('NWC', 'WIO', 'NWC'), feature_group_count=x.shape[-1])
    return y + b.astype(x.dtype)


def sq_relu_mlp(h, w_in, w_out):
    return jnp.square(jax.nn.relu(h @ w_in)) @ w_out


def block_diag(u, w, b):
    ub = u.reshape(u.shape[:-1] + (N_RNN_BLOCKS, RNN_BLOCK))
    y = jnp.einsum('bthi,hij->bthj', ub, w.astype(jnp.float32)) + b.astype(jnp.float32)
    return y.reshape(u.shape)


def rglru_coeffs(u, lam, w_a, b_a, w_x, b_x):
    uf = u.astype(jnp.float32)
    r = jax.nn.sigmoid(block_diag(uf, w_a, b_a))
    i = jax.nn.sigmoid(block_diag(uf, w_x, b_x))
    log_a = -RG_C * r * jax.nn.softplus(-lam.astype(jnp.float32))
    a = jnp.exp(log_a)
    return a, jnp.sqrt(-jnp.expm1(2.0 * log_a)) * (i * uf)


def linear_scan(a, b, h0, reverse):
    def combine(l, r):
        al, bl = l
        ar, br = r
        return al * ar, ar * bl + br
    a_cum, b_cum = lax.associative_scan(combine, (a, b), axis=1, reverse=reverse)
    return a_cum * h0[:, None, :] + b_cum


def recurrent_block(h_lat, h_ctx, w_in, conv_w, conv_b, lam, w_a, b_a, w_x, b_x, w_out, ctx_out):
    w_gate, w_rec = w_in[:, :D_RNN], w_in[:, D_RNN:]
    u_lat = dwconv(h_lat @ w_rec, conv_w, conv_b, REC_CONV_PAD)
    u_ctx = dwconv(h_ctx @ w_rec, conv_w, conv_b, REC_CONV_PAD)
    zeros = jnp.zeros((h_lat.shape[0], D_RNN), jnp.float32)
    ys_lat, ys_ctx = [], []
    for d, rev in enumerate((False, True)):
        a_c, b_c = rglru_coeffs(u_ctx, lam[d], w_a[d], b_a[d], w_x[d], b_x[d])
        s_ctx = linear_scan(a_c, b_c, zeros, rev)
        h0 = s_ctx[:, 0] if rev else s_ctx[:, -1]
        a_l, b_l = rglru_coeffs(u_lat, lam[d], w_a[d], b_a[d], w_x[d], b_x[d])
        ys_lat.append(linear_scan(a_l, b_l, h0, rev))
        ys_ctx.append(s_ctx)
    y_lat = (ys_lat[0] + ys_lat[1]).astype(h_lat.dtype)
    out_lat = (jax.nn.gelu(h_lat @ w_gate) * y_lat) @ w_out
    if not ctx_out:
        return out_lat, None
    y_ctx = (ys_ctx[0] + ys_ctx[1]).astype(h_ctx.dtype)
    out_ctx = (jax.nn.gelu(h_ctx @ w_gate) * y_ctx) @ w_out
    return out_lat, out_ctx


def conformer_conv(h, w_pw1, b_pw1, conv_w, conv_b, ln_g, ln_b, w_pw2, b_pw2):
    z = jax.nn.glu(h @ w_pw1 + b_pw1, axis=-1)
    z = dwconv(z, conv_w, conv_b, CONF_PAD)
    z = jax.nn.silu(layernorm(z, ln_g, ln_b))
    return z @ w_pw2 + b_pw2


def setup_inputs(seed: int = 0) -> dict:
    key = jax.random.key(seed)
    ks = jax.random.split(key, 32)
    f32 = jnp.float32

    def nrm(k, shape, scale):
        return jax.random.normal(k, shape, f32) * scale

    x = nrm(ks[0], (BATCH, SEQ, D_MODEL), 1.0)
    c = nrm(ks[1], (BATCH, D_MODEL), 1.0)
    ctx = nrm(ks[2], (BATCH, CTX_LEN, D_MODEL), 1.0)
    c_ctx = nrm(ks[3], (D_MODEL,), 1.0)
    w_ada = nrm(ks[4], (DEPTH, D_MODEL, N_MOD * D_MODEL), 0.5 * D_MODEL ** -0.5)
    b_ada = nrm(ks[5], (DEPTH, N_MOD * D_MODEL), 0.02)
    norm_g = 1.0 + nrm(ks[6], (DEPTH, 2, D_MODEL), 0.05)
    rec_w_in = nrm(ks[7], (N_REC_LAYERS, D_MODEL, 2 * D_RNN), D_MODEL ** -0.5)
    rec_conv_w = nrm(ks[8], (N_REC_LAYERS, REC_CONV_W, D_RNN), REC_CONV_W ** -0.5)
    rec_conv_b = nrm(ks[9], (N_REC_LAYERS, D_RNN), 0.02)
    u = jax.random.uniform(ks[10], (N_REC_LAYERS, 2, D_RNN), f32, 0.9, 0.999)
    a_base = u ** (1.0 / RG_C)
    rec_lambda = jnp.log(a_base) - jnp.log1p(-a_base)
    rec_w_a = nrm(ks[11], (N_REC_LAYERS, 2, N_RNN_BLOCKS, RNN_BLOCK, RNN_BLOCK), RNN_BLOCK ** -0.5)
    rec_b_a = nrm(ks[12], (N_REC_LAYERS, 2, N_RNN_BLOCKS, RNN_BLOCK), 0.02)
    rec_w_x = nrm(ks[13], (N_REC_LAYERS, 2, N_RNN_BLOCKS, RNN_BLOCK, RNN_BLOCK), RNN_BLOCK ** -0.5)
    rec_b_x = nrm(ks[14], (N_REC_LAYERS, 2, N_RNN_BLOCKS, RNN_BLOCK), 0.02)
    rec_w_out = nrm(ks[15], (N_REC_LAYERS, D_RNN, D_MODEL), D_RNN ** -0.5)
    conf_w_pw1 = nrm(ks[16], (N_CONV_LAYERS, D_MODEL, 2 * D_MODEL), D_MODEL ** -0.5)
    conf_b_pw1 = nrm(ks[17], (N_CONV_LAYERS, 2 * D_MODEL), 0.02)
    conf_conv_w = nrm(ks[18], (N_CONV_LAYERS, CONF_KW, D_MODEL), CONF_KW ** -0.5)
    conf_conv_b = nrm(ks[19], (N_CONV_LAYERS, D_MODEL), 0.02)
    conf_ln_g = 1.0 + nrm(ks[20], (N_CONV_LAYERS, D_MODEL), 0.05)
    conf_ln_b = nrm(ks[21], (N_CONV_LAYERS, D_MODEL), 0.02)
    conf_w_pw2 = nrm(ks[22], (N_CONV_LAYERS, D_MODEL, D_MODEL), D_MODEL ** -0.5)
    conf_b_pw2 = nrm(ks[23], (N_CONV_LAYERS, D_MODEL), 0.02)
    mlp_w_in = nrm(ks[24], (DEPTH, D_MODEL, D_FF), D_MODEL ** -0.5)
    mlp_w_out = nrm(ks[25], (DEPTH, D_FF, D_MODEL), D_FF ** -0.5)
    final_g = 1.0 + nrm(ks[26], (D_MODEL,), 0.05)
    return {'x': x, 'c': c, 'ctx': ctx, 'c_ctx': c_ctx, 'w_ada': w_ada, 'b_ada': b_ada, 'norm_g': norm_g,
            'rec_w_in': rec_w_in, 'rec_conv_w': rec_conv_w, 'rec_conv_b': rec_conv_b, 'rec_lambda': rec_lambda,
            'rec_w_a': rec_w_a, 'rec_b_a': rec_b_a, 'rec_w_x': rec_w_x, 'rec_b_x': rec_b_x, 'rec_w_out': rec_w_out,
            'conf_w_pw1': conf_w_pw1, 'conf_b_pw1': conf_b_pw1, 'conf_conv_w': conf_conv_w, 'conf_conv_b': conf_conv_b,
            'conf_ln_g': conf_ln_g, 'conf_ln_b': conf_ln_b, 'conf_w_pw2': conf_w_pw2, 'conf_b_pw2': conf_b_pw2,
            'mlp_w_in': mlp_w_in, 'mlp_w_out': mlp_w_out, 'final_g': final_g}


def reference(x, c, ctx, c_ctx, w_ada, b_ada, norm_g, rec_w_in, rec_conv_w, rec_conv_b, rec_lambda,
              rec_w_a, rec_b_a, rec_w_x, rec_b_x, rec_w_out, conf_w_pw1, conf_b_pw1, conf_conv_w, conf_conv_b,
              conf_ln_g, conf_ln_b, conf_w_pw2, conf_b_pw2, mlp_w_in, mlp_w_out, final_g):
    rows = x.shape[1] // GRID_W
    x = x + grid_pos_embed(rows, x.shape[-1], x.dtype)[None]
    xc = ctx
    last_ctx_layer = ((DEPTH - 1) // N_MIXERS) * N_MIXERS
    s_c = jax.nn.silu(c)
    s_cc = jax.nn.silu(c_ctx)
    for i in range(DEPTH):
        sh1, sc1, g1, sh2, sc2, g2 = jnp.split((s_c @ w_ada[i] + b_ada[i])[:, None, :], N_MOD, axis=-1)
        use_ctx = i <= last_ctx_layer
        ctx_out = i < last_ctx_layer
        h = modulate(rmsnorm(x, norm_g[i, 0]), sh1, sc1)
        hc = None
        if use_ctx:
            csh1, csc1, cg1, csh2, csc2, cg2 = jnp.split((s_cc @ w_ada[i] + b_ada[i])[None, None, :], N_MOD, axis=-1)
            hc = modulate(rmsnorm(xc, norm_g[i, 0]), csh1, csc1)
        j = i // N_MIXERS
        if i % N_MIXERS == 0:
            y, yc = recurrent_block(h, hc, rec_w_in[j], rec_conv_w[j], rec_conv_b[j], rec_lambda[j],
                                    rec_w_a[j], rec_b_a[j], rec_w_x[j], rec_b_x[j], rec_w_out[j], ctx_out)
        else:
            conf_p = (conf_w_pw1[j], conf_b_pw1[j], conf_conv_w[j], conf_conv_b[j],
                      conf_ln_g[j], conf_ln_b[j], conf_w_pw2[j], conf_b_pw2[j])
            y = conformer_conv(h, *conf_p)
            yc = conformer_conv(hc, *conf_p) if ctx_out else None
        x = x + g1 * y
        x = x + g2 * sq_relu_mlp(modulate(rmsnorm(x, norm_g[i, 1]), sh2, sc2), mlp_w_in[i], mlp_w_out[i])
        if ctx_out:
            xc = xc + cg1 * yc
            xc = xc + cg2 * sq_relu_mlp(modulate(rmsnorm(xc, norm_g[i, 1]), csh2, csc2), mlp_w_in[i], mlp_w_out[i])
    return rmsnorm(x, final_g)
```

```python
import functools

import jax
import jax.numpy as jnp
import numpy as np
from jax import lax
from jax.experimental import pallas as pl
from jax.experimental.pallas import tpu as pltpu

F32 = jnp.float32
BF16 = jnp.bfloat16

D_MODEL = 1024
BATCH = 8
SEQ = 2048
CTX_LEN = 256
GRID_W = 64
D_RNN = 1280
N_RNN_BLOCKS = 16
RNN_BLOCK = D_RNN // N_RNN_BLOCKS
REC_CONV_W = 4
RG_C = 8.0
CONF_KW = 31
CONF_HALF = CONF_KW // 2
D_FF = 4 * D_MODEL
N_MOD = 6
EPS = 1e-6
POS_BASE = 10000.0

VMEM_LIMIT_BYTES = 56 * 1024 * 1024
LANES = 128
MXU_N = 256
BD_WIN = 512
BD_K0 = (0, 128, 384, 640, 768)
N_BD = D_RNN // MXU_N

TT = 64
ROWS = TT * BATCH
HALO = 16
CONF_HALO = 128


def _cparams():
    return pltpu.CompilerParams(dimension_semantics=("arbitrary",), vmem_limit_bytes=VMEM_LIMIT_BYTES)


def _const_spec(shape):
    return pl.BlockSpec(shape, lambda *_: (0,) * len(shape), pipeline_mode=pl.Buffered(1))


def _mod_spec(layer, ctx, k):
    return pl.BlockSpec((None, BATCH, D_MODEL), lambda *_: (layer, int(ctx), k), pipeline_mode=pl.Buffered(1))


def _sigmoid(x):
    return 0.5 * jnp.tanh(0.5 * x) + 0.5


def _rms(x3, g):
    ms = jnp.mean(x3 * x3, axis=-1, keepdims=True)
    return x3 * lax.rsqrt(ms + EPS) * g


def _norm_mod(x3, g, sh, sc):
    return _rms(x3, g) * (1.0 + sc) + sh


def _ada_kernel(c_ref, w_ref, b_ref, o_ref):
    c = c_ref[...]
    s = (c * jax.nn.sigmoid(c)).astype(BF16)
    o_ref[...] = jnp.dot(s, w_ref[...].astype(BF16), preferred_element_type=F32) + b_ref[...]


def _ada_table(cc, w_ada, b_ada):
    depth = w_ada.shape[0]
    tn = 1536
    return pl.pallas_call(
        _ada_kernel,
        grid=(depth, N_MOD * D_MODEL // tn),
        in_specs=[
            pl.BlockSpec((2 * BATCH, D_MODEL), lambda l, j: (0, 0)),
            pl.BlockSpec((None, D_MODEL, tn), lambda l, j: (l, 0, j)),
            pl.BlockSpec((None, 1, tn), lambda l, j: (l, 0, j)),
        ],
        out_specs=pl.BlockSpec((None, 2 * BATCH, tn), lambda l, j: (l, 0, j)),
        out_shape=jax.ShapeDtypeStruct((depth, 2 * BATCH, N_MOD * D_MODEL), F32),
        compiler_params=pltpu.CompilerParams(
            dimension_semantics=("arbitrary", "arbitrary"), vmem_limit_bytes=VMEM_LIMIT_BYTES),
        name="ada_table",
    )(cc, w_ada, b_ada.reshape(depth, 1, N_MOD * D_MODEL))


def _proj_kernel(*refs, nchunks, add_pos, want_gate):
    it = iter(refs)
    xm_ref, xp_ref, xn_ref = next(it), next(it), next(it)
    if add_pos:
        pm_ref, pp_ref, pn_ref = next(it), next(it), next(it)
    g_ref, sh_ref, sc_ref = next(it), next(it), next(it)
    wr_ref, cw_ref, cb_ref = next(it), next(it), next(it)
    if want_gate:
        wg_ref = next(it)
    if add_pos:
        x0_ref = next(it)
    u_ref = next(it)
    if want_gate:
        gate_ref = next(it)
    zr_scr = next(it)

    i = pl.program_id(0)
    nt = TT + 2 * HALO // BATCH
    x = jnp.concatenate([xp_ref[...], xm_ref[...], xn_ref[...]], axis=0)
    x3 = x.reshape(nt, BATCH, D_MODEL)
    if add_pos:
        pos = jnp.concatenate([pp_ref[...], pm_ref[...], pn_ref[...]], axis=0)
        x3 = x3 + pos
        x0_ref[...] = x3[HALO // BATCH:HALO // BATCH + TT].reshape(ROWS, D_MODEL)
    h = _norm_mod(x3, g_ref[...], sh_ref[...], sc_ref[...])
    h = h.reshape(ROWS + 2 * HALO, D_MODEL).astype(BF16)

    zr_scr[...] = jnp.dot(h, wr_ref[...], preferred_element_type=F32)

    @pl.when(i == 0)
    def _():
        zr_scr[0:HALO, :] = jnp.zeros((HALO, D_RNN), F32)

    @pl.when(i == nchunks - 1)
    def _():
        zr_scr[ROWS + HALO:ROWS + 2 * HALO, :] = jnp.zeros((HALO, D_RNN), F32)

    u = cb_ref[...]
    for k in range(REC_CONV_W):
        off = HALO + (k - 1) * BATCH
        u = u + cw_ref[k:k + 1, :] * zr_scr[off:off + ROWS, :]
    u_ref[...] = u

    if want_gate:
        zg = jnp.dot(h[HALO:HALO + ROWS], wg_ref[...], preferred_element_type=F32)
        gate_ref[...] = jax.nn.gelu(zg).astype(BF16)


def _rec_project(xt, pos, mods_layer_ctx, norm_g, w_rec, conv_w, conv_b, w_gate):
    layer, ctx = mods_layer_ctx[1], mods_layer_ctx[2]
    mods = mods_layer_ctx[0]
    n = xt.shape[0]
    nchunks = n // ROWS
    add_pos = pos is not None
    want_gate = w_gate is not None
    hb = ROWS // HALO
    tb = TT // 2

    def prev_idx(i):
        return (jnp.maximum(i * hb - 1, 0), 0)

    def next_idx(i):
        return (jnp.minimum((i + 1) * hb, n // HALO - 1), 0)

    in_specs = [
        pl.BlockSpec((ROWS, D_MODEL), lambda i: (i, 0)),
        pl.BlockSpec((HALO, D_MODEL), prev_idx),
        pl.BlockSpec((HALO, D_MODEL), next_idx),
    ]
    args = [xt, xt, xt]
    if add_pos:
        nblk = pos.shape[0] // 2
        in_specs += [
            pl.BlockSpec((TT, 1, D_MODEL), lambda i: (i, 0, 0)),
            pl.BlockSpec((2, 1, D_MODEL), lambda i: (jnp.maximum(i * tb - 1, 0), 0, 0)),
            pl.BlockSpec((2, 1, D_MODEL), lambda i: (jnp.minimum((i + 1) * tb, nblk - 1), 0, 0)),
        ]
        args += [pos, pos, pos]
    in_specs += [
        _const_spec((1, D_MODEL)),
        _mod_spec(layer, ctx, 0),
        _mod_spec(layer, ctx, 1),
        _const_spec((D_MODEL, D_RNN)),
        _const_spec((REC_CONV_W, D_RNN)),
        _const_spec((1, D_RNN)),
    ]
    args += [norm_g, mods, mods, w_rec, conv_w, conv_b]
    if want_gate:
        in_specs.append(_const_spec((D_MODEL, D_RNN)))
        args.append(w_gate)

    out_specs, out_shape = [], []
    if add_pos:
        out_specs.append(pl.BlockSpec((ROWS, D_MODEL), lambda i: (i, 0)))
        out_shape.append(jax.ShapeDtypeStruct((n, D_MODEL), F32))
    out_specs.append(pl.BlockSpec((ROWS, D_RNN), lambda i: (i, 0)))
    out_shape.append(jax.ShapeDtypeStruct((n, D_RNN), F32))
    if want_gate:
        out_specs.append(pl.BlockSpec((ROWS, D_RNN), lambda i: (i, 0)))
        out_shape.append(jax.ShapeDtypeStruct((n, D_RNN), BF16))

    return pl.pallas_call(
        functools.partial(_proj_kernel, nchunks=nchunks, add_pos=add_pos, want_gate=want_gate),
        grid=(nchunks,),
        in_specs=in_specs,
        out_specs=out_specs,
        out_shape=out_shape,
        scratch_shapes=[pltpu.VMEM((ROWS + 2 * HALO, D_RNN), F32)],
        compiler_params=_cparams(),
        name="rec_project_lat" if add_pos else "rec_project_ctx",
    )(*args)


SCAN_RB = 128


def _scan_kernel(*refs, reverse, mode):
    it = iter(refs)
    u_ref, h0_ref, lam_ref, wab_ref, bab_ref = next(it), next(it), next(it), next(it), next(it)
    if mode == "out":
        yf_ref, gate_ref, x0_ref, g1_ref, wo_ref = next(it), next(it), next(it), next(it), next(it)
        x1_ref = next(it)
    elif mode == "y":
        y_ref = next(it)
    hT_ref = next(it)
    a_scr, b_scr, h_scr = next(it), next(it), next(it)

    i = pl.program_id(0)

    @pl.when(i == 0)
    def _():
        h_scr[...] = h0_ref[...]

    lam = lam_ref[...]
    softplus = jnp.maximum(-lam, 0.0) + jnp.log1p(jnp.exp(-jnp.abs(lam)))
    neg_c_sp = -RG_C * softplus

    for rb in range(ROWS // SCAN_RB):
        rows = slice(rb * SCAN_RB, (rb + 1) * SCAN_RB)
        for j in range(N_BD):
            cols = slice(j * MXU_N, (j + 1) * MXU_N)
            ub = u_ref[rows, BD_K0[j]:BD_K0[j] + BD_WIN].astype(BF16)
            pre = jnp.dot(ub, wab_ref[j], preferred_element_type=F32)
            r = _sigmoid(pre[:, :MXU_N] + bab_ref[0:1, cols])
            ig = _sigmoid(pre[:, MXU_N:] + bab_ref[1:2, cols])
            a = jnp.exp(neg_c_sp[:, cols] * r)
            a_scr[rows, cols] = a
            b_scr[rows, cols] = jnp.sqrt(1.0 - a * a) * (ig * u_ref[rows, cols])

    def step(s, h):
        t = (TT - 1 - s) if reverse else s
        r8 = pl.ds(pl.multiple_of(t * BATCH, BATCH), BATCH)
        h = a_scr[r8, :] * h + b_scr[r8, :]
        b_scr[r8, :] = h
        return h

    h = lax.fori_loop(0, TT, step, h_scr[...], unroll=8)
    h_scr[...] = h
    hT_ref[...] = h

    if mode == "y":
        y_ref[...] = b_scr[...]
    elif mode == "out":
        y = b_scr[...] + yf_ref[...]
        gy = (gate_ref[...].astype(F32) * y).astype(BF16)
        o = jnp.dot(gy, wo_ref[...], preferred_element_type=F32)
        x1 = x0_ref[...].reshape(TT, BATCH, D_MODEL) + g1_ref[...] * o.reshape(TT, BATCH, D_MODEL)
        x1_ref[...] = x1.reshape(ROWS, D_MODEL)


def _rec_scan(u, h0, lam, wab, bab, *, reverse, mode, extra=None, mods=None):
    n = u.shape[0]
    nchunks = n // ROWS
    order = (lambda i: (nchunks - 1 - i, 0)) if reverse else (lambda i: (i, 0))
    in_specs = [
        pl.BlockSpec((ROWS, D_RNN), order),
        _const_spec((BATCH, D_RNN)),
        _const_spec((1, D_RNN)),
        _const_spec((N_BD, BD_WIN, 2 * MXU_N)),
        _const_spec((2, D_RNN)),
    ]
    args = [u, h0, lam, wab, bab]
    out_specs, out_shape = [], []
    if mode == "out":
        yf, gate, x0, w_out = extra
        in_specs += [
            pl.BlockSpec((ROWS, D_RNN), order),
            pl.BlockSpec((ROWS, D_RNN), order),
            pl.BlockSpec((ROWS, D_MODEL), order),
            _mod_spec(0, False, 2),
            _const_spec((D_RNN, D_MODEL)),
        ]
        args += [yf, gate, x0, mods, w_out]
        out_specs.append(pl.BlockSpec((ROWS, D_MODEL), order))
        out_shape.append(jax.ShapeDtypeStruct((n, D_MODEL), F32))
    elif mode == "y":
        out_specs.append(pl.BlockSpec((ROWS, D_RNN), order))
        out_shape.append(jax.ShapeDtypeStruct((n, D_RNN), F32))
    out_specs.append(pl.BlockSpec((BATCH, D_RNN), lambda i: (0, 0)))
    out_shape.append(jax.ShapeDtypeStruct((BATCH, D_RNN), F32))

    return pl.pallas_call(
        functools.partial(_scan_kernel, reverse=reverse, mode=mode),
        grid=(nchunks,),
        in_specs=in_specs,
        out_specs=out_specs,
        out_shape=out_shape,
        scratch_shapes=[
            pltpu.VMEM((ROWS, D_RNN), F32),
            pltpu.VMEM((ROWS, D_RNN), F32),
            pltpu.VMEM((BATCH, D_RNN), F32),
        ],
        compiler_params=_cparams(),
        name="rec_scan_%s_%s" % ("rev" if reverse else "fwd", mode),
    )(*args)


def _mlp_kernel(*refs, final):
    it = iter(refs)
    x_ref, g_ref, sh_ref, sc_ref, gt_ref, w1_ref, w2_ref = (next(it) for _ in range(7))
    if final:
        fg_ref = next(it)
    o_ref = next(it)

    x3 = x_ref[...].reshape(TT, BATCH, D_MODEL)
    h = _norm_mod(x3, g_ref[...], sh_ref[...], sc_ref[...]).reshape(ROWS, D_MODEL).astype(BF16)
    m = jnp.dot(h, w1_ref[...], preferred_element_type=F32)
    m = jnp.square(jnp.maximum(m, 0.0)).astype(BF16)
    o = jnp.dot(m, w2_ref[...], preferred_element_type=F32)
    y = x3 + gt_ref[...] * o.reshape(TT, BATCH, D_MODEL)
    if final:
        y = _rms(y, fg_ref[...])
    o_ref[...] = y.reshape(ROWS, D_MODEL)


def _mlp(x, mods, layer, norm_g, w1, w2, final_g=None):
    n = x.shape[0]
    final = final_g is not None
    in_specs = [
        pl.BlockSpec((ROWS, D_MODEL), lambda i: (i, 0)),
        _const_spec((1, D_MODEL)),
        _mod_spec(layer, False, 3),
        _mod_spec(layer, False, 4),
        _mod_spec(layer, False, 5),
        _const_spec((D_MODEL, D_FF)),
        _const_spec((D_FF, D_MODEL)),
    ]
    args = [x, norm_g, mods, mods, mods, w1, w2]
    if final:
        in_specs.append(_const_spec((1, D_MODEL)))
        args.append(final_g)
    return pl.pallas_call(
        functools.partial(_mlp_kernel, final=final),
        grid=(n // ROWS,),
        in_specs=in_specs,
        out_specs=pl.BlockSpec((ROWS, D_MODEL), lambda i: (i, 0)),
        out_shape=jax.ShapeDtypeStruct((n, D_MODEL), F32),
        compiler_params=_cparams(),
        name="mlp_final" if final else "mlp",
    )(*args)


def _conf_in_kernel(x_ref, g_ref, sh_ref, sc_ref, w_ref, b_ref, z_ref):
    x3 = x_ref[...].reshape(TT, BATCH, D_MODEL)
    h = _norm_mod(x3, g_ref[...], sh_ref[...], sc_ref[...]).reshape(ROWS, D_MODEL).astype(BF16)
    z = jnp.dot(h, w_ref[...], preferred_element_type=F32) + b_ref[...]
    z_ref[...] = z[:, :D_MODEL] * _sigmoid(z[:, D_MODEL:])


def _conf_in(x, mods, layer, norm_g, w_pw1, b_pw1):
    n = x.shape[0]
    return pl.pallas_call(
        _conf_in_kernel,
        grid=(n // ROWS,),
        in_specs=[
            pl.BlockSpec((ROWS, D_MODEL), lambda i: (i, 0)),
            _const_spec((1, D_MODEL)),
            _mod_spec(layer, False, 0),
            _mod_spec(layer, False, 1),
            _const_spec((D_MODEL, 2 * D_MODEL)),
            _const_spec((1, 2 * D_MODEL)),
        ],
        out_specs=pl.BlockSpec((ROWS, D_MODEL), lambda i: (i, 0)),
        out_shape=jax.ShapeDtypeStruct((n, D_MODEL), F32),
        compiler_params=_cparams(),
        name="conf_in",
    )(x, norm_g, mods, mods, w_pw1, b_pw1)


CONV_RB = 64


def _conf_out_kernel(zm_ref, zp_ref, zn_ref, x_ref, cw_ref, cb_ref, lg_ref, lb_ref, w_ref, b_ref, g1_ref,
                     o_ref, zext_scr, conv_scr, *, nchunks):
    i = pl.program_id(0)
    zext_scr[0:CONF_HALO, :] = zp_ref[...]
    zext_scr[CONF_HALO:CONF_HALO + ROWS, :] = zm_ref[...]
    zext_scr[CONF_HALO + ROWS:, :] = zn_ref[...]

    @pl.when(i == 0)
    def _():
        zext_scr[0:CONF_HALO, :] = jnp.zeros((CONF_HALO, D_MODEL), F32)

    @pl.when(i == nchunks - 1)
    def _():
        zext_scr[CONF_HALO + ROWS:, :] = jnp.zeros((CONF_HALO, D_MODEL), F32)

    for c in range(D_MODEL // LANES):
        lanes = slice(c * LANES, (c + 1) * LANES)
        wk = [cw_ref[k:k + 1, lanes] for k in range(CONF_KW)]
        bias = cb_ref[:, lanes]

        def rows_body(rb, carry, lanes=lanes, wk=wk, bias=bias):
            base = pl.multiple_of(rb * CONV_RB, CONV_RB)
            acc = jnp.broadcast_to(bias, (CONV_RB, LANES))
            for k in range(CONF_KW):
                off = CONF_HALO + (k - CONF_HALF) * BATCH
                acc = acc + wk[k] * zext_scr[pl.ds(base + off, CONV_RB), lanes]
            conv_scr[pl.ds(base, CONV_RB), lanes] = acc
            return carry

        lax.fori_loop(0, ROWS // CONV_RB, rows_body, 0)

    z = conv_scr[...]
    mu = jnp.mean(z, axis=-1, keepdims=True)
    zc = z - mu
    var = jnp.mean(zc * zc, axis=-1, keepdims=True)
    zl = zc * lax.rsqrt(var + EPS) * lg_ref[...] + lb_ref[...]
    zs = (zl * _sigmoid(zl)).astype(BF16)
    y = jnp.dot(zs, w_ref[...], preferred_element_type=F32) + b_ref[...]
    x1 = x_ref[...].reshape(TT, BATCH, D_MODEL) + g1_ref[...] * y.reshape(TT, BATCH, D_MODEL)
    o_ref[...] = x1.reshape(ROWS, D_MODEL)


def _conf_out(z, x, mods, layer, conv_w, conv_b, ln_g, ln_b, w_pw2, b_pw2):
    n = x.shape[0]
    nchunks = n // ROWS
    hb = ROWS // CONF_HALO
    return pl.pallas_call(
        functools.partial(_conf_out_kernel, nchunks=nchunks),
        grid=(nchunks,),
        in_specs=[
            pl.BlockSpec((ROWS, D_MODEL), lambda i: (i, 0)),
            pl.BlockSpec((CONF_HALO, D_MODEL), lambda i: (jnp.maximum(i * hb - 1, 0), 0)),
            pl.BlockSpec((CONF_HALO, D_MODEL), lambda i: (jnp.minimum((i + 1) * hb, n // CONF_HALO - 1), 0)),
            pl.BlockSpec((ROWS, D_MODEL), lambda i: (i, 0)),
            _const_spec((CONF_KW, D_MODEL)),
            _const_spec((1, D_MODEL)),
            _const_spec((1, D_MODEL)),
            _const_spec((1, D_MODEL)),
            _const_spec((D_MODEL, D_MODEL)),
            _const_spec((1, D_MODEL)),
            _mod_spec(layer, False, 2),
        ],
        out_specs=pl.BlockSpec((ROWS, D_MODEL), lambda i: (i, 0)),
        out_shape=jax.ShapeDtypeStruct((n, D_MODEL), F32),
        scratch_shapes=[
            pltpu.VMEM((ROWS + 2 * CONF_HALO, D_MODEL), F32),
            pltpu.VMEM((ROWS, D_MODEL), F32),
        ],
        compiler_params=_cparams(),
        name="conf_out",
    )(z, z, z, x, conv_w, conv_b, ln_g, ln_b, w_pw2, b_pw2, mods)


def _pos_table(rows, d):
    t = jnp.arange(rows * GRID_W, dtype=jnp.int32)
    row = (t // GRID_W).astype(F32)
    col = (t % GRID_W).astype(F32)
    q = d // 4
    omega = 1.0 / (POS_BASE ** (jnp.arange(q, dtype=F32) / q))
    er = row[:, None] * omega[None, :]
    ec = col[:, None] * omega[None, :]
    return jnp.concatenate([jnp.sin(er), jnp.cos(er), jnp.sin(ec), jnp.cos(ec)], axis=-1)


def _block_diag_windows(w_a, w_x):
    eye = jnp.eye(N_RNN_BLOCKS, dtype=F32)

    def dense(w):
        return (w[:, :, None, :] * eye[:, None, :, None]).reshape(D_RNN, D_RNN)

    da, dx = dense(w_a), dense(w_x)
    wins = []
    for j, k0 in enumerate(BD_K0):
        cols = slice(j * MXU_N, (j + 1) * MXU_N)
        wins.append(jnp.concatenate([da[k0:k0 + BD_WIN, cols], dx[k0:k0 + BD_WIN, cols]], axis=1))
    return jnp.stack(wins).astype(BF16)


def _time_major(a):
    b, t, d = a.shape
    return jnp.transpose(a, (1, 0, 2)).reshape(t * b, d)


def kernel(x, c, ctx, c_ctx, w_ada, b_ada, norm_g, rec_w_in, rec_conv_w, rec_conv_b, rec_lambda, rec_w_a, rec_b_a,
           rec_w_x, rec_b_x, rec_w_out, conf_w_pw1, conf_b_pw1, conf_conv_w, conf_conv_b, conf_ln_g, conf_ln_b,
           conf_w_pw2, conf_b_pw2, mlp_w_in, mlp_w_out, final_g):
    assert x.shape == (BATCH, SEQ, D_MODEL) and ctx.shape == (BATCH, CTX_LEN, D_MODEL)
    assert w_ada.shape[0] == 2, "one recurrent layer followed by one conformer layer"

    cc = jnp.concatenate([c, jnp.broadcast_to(c_ctx[None, :], (BATCH, D_MODEL))], axis=0)
    mods = _ada_table(cc, w_ada, b_ada)

    xt = _time_major(x)
    ct = _time_major(ctx)
    pos = _pos_table(SEQ // GRID_W, D_MODEL).reshape(SEQ, 1, D_MODEL)

    w_in = rec_w_in[0].astype(BF16)
    w_gate, w_rec = w_in[:, :D_RNN], w_in[:, D_RNN:]
    conv_w, conv_b = rec_conv_w[0], rec_conv_b[0].reshape(1, D_RNN)
    g0 = norm_g[0, 0].reshape(1, D_MODEL)
    wab = [_block_diag_windows(rec_w_a[0, d], rec_w_x[0, d]) for d in range(2)]
    bab = [jnp.stack([rec_b_a[0, d].reshape(D_RNN), rec_b_x[0, d].reshape(D_RNN)]) for d in range(2)]
    lam = [rec_lambda[0, d].reshape(1, D_RNN) for d in range(2)]
    zeros = jnp.zeros((BATCH, D_RNN), F32)

    (u_ctx,) = _rec_project(ct, None, (mods, 0, True), g0, w_rec, conv_w, conv_b, None)
    (h0_f,) = _rec_scan(u_ctx, zeros, lam[0], wab[0], bab[0], reverse=False, mode="state")
    (h0_r,) = _rec_scan(u_ctx, zeros, lam[1], wab[1], bab[1], reverse=True, mode="state")

    x0, u_lat, gate = _rec_project(xt, pos, (mods, 0, False), g0, w_rec, conv_w, conv_b, w_gate)
    y_f, _ = _rec_scan(u_lat, h0_f, lam[0], wab[0], bab[0], reverse=False, mode="y")
    x1, _ = _rec_scan(u_lat, h0_r, lam[1], wab[1], bab[1], reverse=True, mode="out",
                      extra=(y_f, gate, x0, rec_w_out[0].astype(BF16)), mods=mods)
    x2 = _mlp(x1, mods, 0, norm_g[0, 1].reshape(1, D_MODEL), mlp_w_in[0].astype(BF16), mlp_w_out[0].astype(BF16))

    z = _conf_in(x2, mods, 1, norm_g[1, 0].reshape(1, D_MODEL), conf_w_pw1[0].astype(BF16),
                 conf_b_pw1[0].reshape(1, 2 * D_MODEL))
    x3 = _conf_out(z, x2, mods, 1, conf_conv_w[0], conf_conv_b[0].reshape(1, D_MODEL),
                   conf_ln_g[0].reshape(1, D_MODEL), conf_ln_b[0].reshape(1, D_MODEL),
                   conf_w_pw2[0].astype(BF16), conf_b_pw2[0].reshape(1, D_MODEL))
    out = _mlp(x3, mods, 1, norm_g[1, 1].reshape(1, D_MODEL), mlp_w_in[1].astype(BF16), mlp_w_out[1].astype(BF16),
               final_g=final_g.reshape(1, D_MODEL))

    return jnp.transpose(out.reshape(SEQ, BATCH, D_MODEL), (1, 0, 2))
```

```python
import functools

import jax
import jax.numpy as jnp
from jax import lax
from jax.experimental import pallas as pl
from jax.experimental.pallas import tpu as pltpu

F32 = jnp.float32
BF16 = jnp.bfloat16

D_MODEL = 1024
BATCH = 8
SEQ = 2048
CTX_LEN = 256
GRID_W = 64
D_RNN = 1280
N_RNN_BLOCKS = 16
RNN_BLOCK = D_RNN // N_RNN_BLOCKS
REC_CONV_W = 4
RG_C = 8.0
CONF_KW = 31
CONF_HALF = CONF_KW // 2
D_FF = 4 * D_MODEL
N_MOD = 6
EPS = 1e-6
POS_BASE = 10000.0

VMEM_LIMIT_BYTES = 56 * 1024 * 1024
LANES = 128
MXU_N = 256
BD_WIN = 512
BD_K0 = (0, 128, 384, 640, 768)
N_BD = D_RNN // MXU_N

TT = 64
ROWS = TT * BATCH
HALO = 16
HALO_T = HALO // BATCH
XHALO_T = 8
CONF_HALO = 128


def _cparams():
    return pltpu.CompilerParams(dimension_semantics=("arbitrary",), vmem_limit_bytes=VMEM_LIMIT_BYTES)


def _const_spec(shape):
    return pl.BlockSpec(shape, lambda *_: (0,) * len(shape), pipeline_mode=pl.Buffered(1))


def _layer_spec(shape, *idx):
    return pl.BlockSpec((None,) + tuple(shape), lambda *_: tuple(idx), pipeline_mode=pl.Buffered(1))


def _mod_spec(layer, ctx, k):
    return _layer_spec((BATCH, D_MODEL), layer, int(ctx), k)


def _sigmoid(x):
    return 0.5 * jnp.tanh(0.5 * x) + 0.5


def _rms(x3, g):
    ms = jnp.mean(x3 * x3, axis=-1, keepdims=True)
    return x3 * lax.rsqrt(ms + EPS) * g


def _norm_mod(x3, g, sh, sc):
    return _rms(x3, g) * (1.0 + sc) + sh


def _ada_kernel(c_ref, w_ref, b_ref, o_ref):
    c = c_ref[...]
    s = (c * jax.nn.sigmoid(c)).astype(BF16)
    o_ref[...] = jnp.dot(s, w_ref[...].astype(BF16), preferred_element_type=F32) + b_ref[...]


def _ada_table(cc, w_ada, b_ada):
    depth = w_ada.shape[0]
    tn = 1536
    return pl.pallas_call(
        _ada_kernel,
        grid=(depth, N_MOD * D_MODEL // tn),
        in_specs=[
            pl.BlockSpec((2 * BATCH, D_MODEL), lambda l, j: (0, 0)),
            pl.BlockSpec((None, D_MODEL, tn), lambda l, j: (l, 0, j)),
            pl.BlockSpec((None, 1, tn), lambda l, j: (l, 0, j)),
        ],
        out_specs=pl.BlockSpec((None, 2 * BATCH, tn), lambda l, j: (l, 0, j)),
        out_shape=jax.ShapeDtypeStruct((depth, 2 * BATCH, N_MOD * D_MODEL), F32),
        compiler_params=pltpu.CompilerParams(
            dimension_semantics=("arbitrary", "arbitrary"), vmem_limit_bytes=VMEM_LIMIT_BYTES),
        name="ada_table",
    )(cc, w_ada, b_ada.reshape(depth, 1, N_MOD * D_MODEL))


def _fir(src_ref, dst_ref, w_ref, b_ref, *, n_taps, src_off, n_out, width, block):
    for c in range(width // LANES):
        lanes = slice(c * LANES, (c + 1) * LANES)
        wk = [jnp.broadcast_to(w_ref[k:k + 1, lanes], (BATCH, LANES)) for k in range(n_taps)]
        bias = jnp.broadcast_to(b_ref[:, lanes], (BATCH, LANES))

        def body(blk, carry):
            base = pl.multiple_of(blk * (block * BATCH), block * BATCH)
            acc = [bias] * block
            for q in range(block + n_taps - 1):
                z = src_ref[pl.ds(base + src_off + q * BATCH, BATCH), lanes]
                for j in range(max(0, q - n_taps + 1), min(block, q + 1)):
                    acc[j] = acc[j] + wk[q - j] * z
            for j in range(block):
                dst_ref[pl.ds(base + j * BATCH, BATCH), lanes] = acc[j]
            return carry

        lax.fori_loop(0, n_out // block, body, 0)


SCAN_RB = 128
LOG2_E = 1.4426950408889634


def _rglru_coeffs(u_ref, ub_scr, lam_ref, wab_ref, bab_ref, a_scr, b_scr):
    lam = lam_ref[...]
    softplus = jnp.maximum(-lam, 0.0) + jnp.log1p(jnp.exp(-jnp.abs(lam)))
    k = (-0.5 * RG_C * LOG2_E) * softplus
    ub_scr[...] = u_ref[...].astype(BF16)
    for rb in range(ROWS // SCAN_RB):
        rows = slice(rb * SCAN_RB, (rb + 1) * SCAN_RB)
        for j in range(N_BD):
            cols = slice(j * MXU_N, (j + 1) * MXU_N)
            pre = jnp.dot(ub_scr[rows, BD_K0[j]:BD_K0[j] + BD_WIN], wab_ref[j],
                          preferred_element_type=F32)
            t_r = jnp.tanh(pre[:, :MXU_N] + bab_ref[0:1, cols])
            t_i = jnp.tanh(pre[:, MXU_N:] + bab_ref[1:2, cols])
            a = jnp.exp2(k[:, cols] * t_r + k[:, cols])
            w = 1.0 - a * a
            sq = jnp.where(w > 0.0, w * lax.rsqrt(w), 0.0)
            a_scr[rows, cols] = a
            b_scr[rows, cols] = (0.5 * sq) * ((t_i + 1.0) * u_ref[rows, cols])


def _rglru_scan(a_scr, b_scr, h_scr, reverse):
    def step(s, h):
        t = (TT - 1 - s) if reverse else s
        r8 = pl.ds(pl.multiple_of(t * BATCH, BATCH), BATCH)
        h = a_scr[r8, :] * h + b_scr[r8, :]
        b_scr[r8, :] = h
        return h

    h = lax.fori_loop(0, TT, step, h_scr[...], unroll=8)
    h_scr[...] = h
    return h


def _proj_kernel(*refs, nchunks, add_pos, want_gate):
    it = iter(refs)
    xm_ref, xp_ref, xn_ref = next(it), next(it), next(it)
    if add_pos:
        rowtab_ref, coltab_ref = next(it), next(it)
    g_ref, sh_ref, sc_ref = next(it), next(it), next(it)
    wr_ref, cw_ref, cb_ref = next(it), next(it), next(it)
    h0_ref, lam_ref, wab_ref, bab_ref = next(it), next(it), next(it), next(it)
    if want_gate:
        wg_ref = next(it)
    if add_pos:
        x0_ref = next(it)
    u_ref = next(it)
    if want_gate:
        gate_ref = next(it)
        yf_ref = next(it)
    hT_ref = next(it)
    zr_scr, ub_scr, a_scr, b_scr, h_scr = next(it), next(it), next(it), next(it), next(it)

    i = pl.program_id(0)

    @pl.when(i == 0)
    def _():
        h_scr[...] = h0_ref[...]

    xm = jnp.swapaxes(xm_ref[...], 0, 1)
    xp = jnp.swapaxes(xp_ref[...], 0, 1)[XHALO_T - HALO_T:]
    xn = jnp.swapaxes(xn_ref[...], 0, 1)[:HALO_T]
    x3 = jnp.concatenate([xp, xm, xn], axis=0)
    if add_pos:
        half = D_MODEL // 2
        r_prev = rowtab_ref[pl.ds(jnp.maximum(i - 1, 0), 1)]
        r_here = rowtab_ref[pl.ds(i, 1)]
        r_next = rowtab_ref[pl.ds(jnp.minimum(i + 1, nchunks - 1), 1)]
        prow = jnp.concatenate([jnp.broadcast_to(r_prev, (HALO_T, 1, half)),
                                jnp.broadcast_to(r_here, (TT, 1, half)),
                                jnp.broadcast_to(r_next, (HALO_T, 1, half))], axis=0)
        pcol = jnp.concatenate([coltab_ref[GRID_W - HALO_T:], coltab_ref[...], coltab_ref[:HALO_T]], axis=0)
        x3 = x3 + jnp.concatenate([prow, pcol], axis=-1)
        x0_ref[...] = x3[HALO_T:HALO_T + TT].reshape(ROWS, D_MODEL)
    h = _norm_mod(x3, g_ref[...], sh_ref[...], sc_ref[...])
    h = h.reshape(ROWS + 2 * HALO, D_MODEL).astype(BF16)

    zr_scr[...] = jnp.dot(h, wr_ref[...], preferred_element_type=F32)

    @pl.when(i == 0)
    def _():
        zr_scr[0:HALO, :] = jnp.zeros((HALO, D_RNN), F32)

    @pl.when(i == nchunks - 1)
    def _():
        zr_scr[ROWS + HALO:ROWS + 2 * HALO, :] = jnp.zeros((HALO, D_RNN), F32)

    _fir(zr_scr, u_ref, cw_ref, cb_ref, n_taps=REC_CONV_W, src_off=HALO - BATCH, n_out=TT, width=D_RNN, block=16)

    if want_gate:
        zg = jnp.dot(h[HALO:HALO + ROWS], wg_ref[...], preferred_element_type=F32)
        gate_ref[...] = jax.nn.gelu(zg).astype(BF16)

    _rglru_coeffs(u_ref, ub_scr, lam_ref, wab_ref, bab_ref, a_scr, b_scr)
    hT_ref[...] = _rglru_scan(a_scr, b_scr, h_scr, reverse=False)
    if want_gate:
        yf_ref[...] = b_scr[...]


def _rec_project(x, pos_tabs, mods, ctx, norm_g, w_in, conv_w, conv_b, h0, lam, wab, bab, want_gate):
    t_len = x.shape[1]
    n = t_len * BATCH
    nchunks = t_len // TT
    add_pos = pos_tabs is not None
    hb = TT // XHALO_T
    nhb = t_len // XHALO_T

    in_specs = [
        pl.BlockSpec((BATCH, TT, D_MODEL), lambda i: (0, i, 0)),
        pl.BlockSpec((BATCH, XHALO_T, D_MODEL), lambda i: (0, jnp.maximum(i * hb - 1, 0), 0)),
        pl.BlockSpec((BATCH, XHALO_T, D_MODEL), lambda i: (0, jnp.minimum((i + 1) * hb, nhb - 1), 0)),
    ]
    args = [x, x, x]
    if add_pos:
        assert TT == GRID_W and pos_tabs[0].shape[0] == nchunks
        in_specs += [_const_spec(pos_tabs[0].shape), _const_spec(pos_tabs[1].shape)]
        args += list(pos_tabs)
    in_specs += [
        _layer_spec((1, D_MODEL), 0, 0, 0),
        _mod_spec(0, ctx, 0),
        _mod_spec(0, ctx, 1),
        _layer_spec((D_MODEL, D_RNN), 0, 0, 1),
        _layer_spec((REC_CONV_W, D_RNN), 0, 0, 0),
        _layer_spec((1, D_RNN), 0, 0, 0),
    ]
    args += [norm_g, mods, mods, w_in, conv_w, conv_b]
    in_specs += [
        _const_spec((BATCH, D_RNN)),
        _layer_spec((1, D_RNN), 0, 0, 0),
        _layer_spec((N_BD, BD_WIN, 2 * MXU_N), 0, 0, 0, 0),
        _layer_spec((2, D_RNN), 0, 0, 0),
    ]
    args += [h0, lam, wab, bab]
    if want_gate:
        in_specs.append(_layer_spec((D_MODEL, D_RNN), 0, 0, 0))
        args.append(w_in)

    out_specs, out_shape = [], []
    if add_pos:
        out_specs.append(pl.BlockSpec((ROWS, D_MODEL), lambda i: (i, 0)))
        out_shape.append(jax.ShapeDtypeStruct((n, D_MODEL), F32))
    out_specs.append(pl.BlockSpec((ROWS, D_RNN), lambda i: (i, 0)))
    out_shape.append(jax.ShapeDtypeStruct((n, D_RNN), F32))
    if want_gate:
        out_specs.append(pl.BlockSpec((ROWS, D_RNN), lambda i: (i, 0)))
        out_shape.append(jax.ShapeDtypeStruct((n, D_RNN), BF16))
        out_specs.append(pl.BlockSpec((ROWS, D_RNN), lambda i: (i, 0)))
        out_shape.append(jax.ShapeDtypeStruct((n, D_RNN), F32))
    out_specs.append(pl.BlockSpec((BATCH, D_RNN), lambda i: (0, 0)))
    out_shape.append(jax.ShapeDtypeStruct((BATCH, D_RNN), F32))

    return pl.pallas_call(
        functools.partial(_proj_kernel, nchunks=nchunks, add_pos=add_pos, want_gate=want_gate),
        grid=(nchunks,),
        in_specs=in_specs,
        out_specs=out_specs,
        out_shape=out_shape,
        scratch_shapes=[
            pltpu.VMEM((ROWS + 2 * HALO, D_RNN), F32),
            pltpu.VMEM((ROWS, D_RNN), BF16),
            pltpu.VMEM((ROWS, D_RNN), F32),
            pltpu.VMEM((ROWS, D_RNN), F32),
            pltpu.VMEM((BATCH, D_RNN), F32),
        ],
        compiler_params=_cparams(),
        name="rec_fwd_lat" if add_pos else "rec_fwd_ctx",
    )(*args)


def _scan_kernel(*refs, mode):
    it = iter(refs)
    u_ref, h0_ref, lam_ref, wab_ref, bab_ref = next(it), next(it), next(it), next(it), next(it)
    if mode == "out":
        yf_ref, gate_ref, x0_ref, g1_ref, wo_ref = next(it), next(it), next(it), next(it), next(it)
        x1_ref = next(it)
    hT_ref = next(it)
    ub_scr, a_scr, b_scr, h_scr = next(it), next(it), next(it), next(it)

    i = pl.program_id(0)

    @pl.when(i == 0)
    def _():
        h_scr[...] = h0_ref[...]

    _rglru_coeffs(u_ref, ub_scr, lam_ref, wab_ref, bab_ref, a_scr, b_scr)
    hT_ref[...] = _rglru_scan(a_scr, b_scr, h_scr, reverse=True)

    if mode == "out":
        y = b_scr[...] + yf_ref[...]
        gy = (gate_ref[...].astype(F32) * y).astype(BF16)
        o = jnp.dot(gy, wo_ref[...], preferred_element_type=F32)
        x1 = x0_ref[...].reshape(TT, BATCH, D_MODEL) + g1_ref[...] * o.reshape(TT, BATCH, D_MODEL)
        x1_ref[...] = x1.reshape(ROWS, D_MODEL)


def _rec_scan_rev(u, h0, lam, wab, bab, *, mode, extra=None, mods=None):
    n = u.shape[0]
    nchunks = n // ROWS
    order = lambda i: (nchunks - 1 - i, 0)
    in_specs = [
        pl.BlockSpec((ROWS, D_RNN), order),
        _const_spec((BATCH, D_RNN)),
        _layer_spec((1, D_RNN), 1, 0, 0),
        _layer_spec((N_BD, BD_WIN, 2 * MXU_N), 1, 0, 0, 0),
        _layer_spec((2, D_RNN), 1, 0, 0),
    ]
    args = [u, h0, lam, wab, bab]
    out_specs, out_shape = [], []
    if mode == "out":
        yf, gate, x0, w_out = extra
        in_specs += [
            pl.BlockSpec((ROWS, D_RNN), order),
            pl.BlockSpec((ROWS, D_RNN), order),
            pl.BlockSpec((ROWS, D_MODEL), order),
            _mod_spec(0, False, 2),
            _layer_spec((D_RNN, D_MODEL), 0, 0, 0),
        ]
        args += [yf, gate, x0, mods, w_out]
        out_specs.append(pl.BlockSpec((ROWS, D_MODEL), order))
        out_shape.append(jax.ShapeDtypeStruct((n, D_MODEL), F32))
    out_specs.append(pl.BlockSpec((BATCH, D_RNN), lambda i: (0, 0)))
    out_shape.append(jax.ShapeDtypeStruct((BATCH, D_RNN), F32))

    return pl.pallas_call(
        functools.partial(_scan_kernel, mode=mode),
        grid=(nchunks,),
        in_specs=in_specs,
        out_specs=out_specs,
        out_shape=out_shape,
        scratch_shapes=[
            pltpu.VMEM((ROWS, D_RNN), BF16),
            pltpu.VMEM((ROWS, D_RNN), F32),
            pltpu.VMEM((ROWS, D_RNN), F32),
            pltpu.VMEM((BATCH, D_RNN), F32),
        ],
        compiler_params=_cparams(),
        name="rec_rev_" + mode,
    )(*args)


def _mlp_kernel(*refs, final):
    it = iter(refs)
    x_ref, g_ref, sh_ref, sc_ref, gt_ref, w1_ref, w2_ref = (next(it) for _ in range(7))
    if final:
        fg_ref = next(it)
    o_ref = next(it)

    x3 = x_ref[...].reshape(TT, BATCH, D_MODEL)
    h = _norm_mod(x3, g_ref[...], sh_ref[...], sc_ref[...]).reshape(ROWS, D_MODEL).astype(BF16)
    m = jnp.dot(h, w1_ref[...], preferred_element_type=F32)
    m = jnp.square(jnp.maximum(m, 0.0)).astype(BF16)
    o = jnp.dot(m, w2_ref[...], preferred_element_type=F32)
    y = x3 + gt_ref[...] * o.reshape(TT, BATCH, D_MODEL)
    if final:
        o_ref[...] = jnp.swapaxes(_rms(y, fg_ref[...]), 0, 1)
    else:
        o_ref[...] = y.reshape(ROWS, D_MODEL)


def _mlp(x, mods, layer, norm_g, w1, w2, final_g=None):
    n = x.shape[0]
    final = final_g is not None
    in_specs = [
        pl.BlockSpec((ROWS, D_MODEL), lambda i: (i, 0)),
        _layer_spec((1, D_MODEL), 2 * layer + 1, 0, 0),
        _mod_spec(layer, False, 3),
        _mod_spec(layer, False, 4),
        _mod_spec(layer, False, 5),
        _layer_spec((D_MODEL, D_FF), layer, 0, 0),
        _layer_spec((D_FF, D_MODEL), layer, 0, 0),
    ]
    args = [x, norm_g, mods, mods, mods, w1, w2]
    if final:
        in_specs.append(_const_spec((1, D_MODEL)))
        args.append(final_g)
        out_spec = pl.BlockSpec((BATCH, TT, D_MODEL), lambda i: (0, i, 0))
        out_shape = jax.ShapeDtypeStruct((BATCH, n // BATCH, D_MODEL), F32)
    else:
        out_spec = pl.BlockSpec((ROWS, D_MODEL), lambda i: (i, 0))
        out_shape = jax.ShapeDtypeStruct((n, D_MODEL), F32)
    return pl.pallas_call(
        functools.partial(_mlp_kernel, final=final),
        grid=(n // ROWS,),
        in_specs=in_specs,
        out_specs=out_spec,
        out_shape=out_shape,
        compiler_params=_cparams(),
        name="mlp_final" if final else "mlp",
    )(*args)


def _conf_in_kernel(x_ref, g_ref, sh_ref, sc_ref, w_ref, b_ref, z_ref):
    x3 = x_ref[...].reshape(TT, BATCH, D_MODEL)
    h = _norm_mod(x3, g_ref[...], sh_ref[...], sc_ref[...]).reshape(ROWS, D_MODEL).astype(BF16)
    z = jnp.dot(h, w_ref[...], preferred_element_type=F32) + b_ref[...]
    z_ref[...] = z[:, :D_MODEL] * _sigmoid(z[:, D_MODEL:])


def _conf_in(x, mods, layer, norm_g, w_pw1, b_pw1):
    n = x.shape[0]
    return pl.pallas_call(
        _conf_in_kernel,
        grid=(n // ROWS,),
        in_specs=[
            pl.BlockSpec((ROWS, D_MODEL), lambda i: (i, 0)),
            _layer_spec((1, D_MODEL), 2 * layer, 0, 0),
            _mod_spec(layer, False, 0),
            _mod_spec(layer, False, 1),
            _layer_spec((D_MODEL, 2 * D_MODEL), 0, 0, 0),
            _const_spec((1, 2 * D_MODEL)),
        ],
        out_specs=pl.BlockSpec((ROWS, D_MODEL), lambda i: (i, 0)),
        out_shape=jax.ShapeDtypeStruct((n, D_MODEL), F32),
        compiler_params=_cparams(),
        name="conf_in",
    )(x, norm_g, mods, mods, w_pw1, b_pw1)


def _conf_out_kernel(zm_ref, zp_ref, zn_ref, x_ref, cw_ref, cb_ref, lg_ref, lb_ref, w_ref, b_ref, g1_ref,
                     o_ref, zext_scr, conv_scr, *, nchunks):
    i = pl.program_id(0)
    zext_scr[0:CONF_HALO, :] = zp_ref[...]
    zext_scr[CONF_HALO:CONF_HALO + ROWS, :] = zm_ref[...]
    zext_scr[CONF_HALO + ROWS:, :] = zn_ref[...]

    @pl.when(i == 0)
    def _():
        zext_scr[0:CONF_HALO, :] = jnp.zeros((CONF_HALO, D_MODEL), F32)

    @pl.when(i == nchunks - 1)
    def _():
        zext_scr[CONF_HALO + ROWS:, :] = jnp.zeros((CONF_HALO, D_MODEL), F32)

    _fir(zext_scr, conv_scr, cw_ref, cb_ref, n_taps=CONF_KW, src_off=CONF_HALO - CONF_HALF * BATCH,
         n_out=TT, width=D_MODEL, block=16)

    z = conv_scr[...]
    mu = jnp.mean(z, axis=-1, keepdims=True)
    zc = z - mu
    var = jnp.mean(zc * zc, axis=-1, keepdims=True)
    zl = zc * lax.rsqrt(var + EPS) * lg_ref[...] + lb_ref[...]
    zs = (zl * _sigmoid(zl)).astype(BF16)
    y = jnp.dot(zs, w_ref[...], preferred_element_type=F32) + b_ref[...]
    x1 = x_ref[...].reshape(TT, BATCH, D_MODEL) + g1_ref[...] * y.reshape(TT, BATCH, D_MODEL)
    o_ref[...] = x1.reshape(ROWS, D_MODEL)


def _conf_out(z, x, mods, layer, conv_w, conv_b, ln_g, ln_b, w_pw2, b_pw2):
    n = x.shape[0]
    nchunks = n // ROWS
    hb = ROWS // CONF_HALO
    return pl.pallas_call(
        functools.partial(_conf_out_kernel, nchunks=nchunks),
        grid=(nchunks,),
        in_specs=[
            pl.BlockSpec((ROWS, D_MODEL), lambda i: (i, 0)),
            pl.BlockSpec((CONF_HALO, D_MODEL), lambda i: (jnp.maximum(i * hb - 1, 0), 0)),
            pl.BlockSpec((CONF_HALO, D_MODEL), lambda i: (jnp.minimum((i + 1) * hb, n // CONF_HALO - 1), 0)),
            pl.BlockSpec((ROWS, D_MODEL), lambda i: (i, 0)),
            _layer_spec((CONF_KW, D_MODEL), 0, 0, 0),
            _const_spec((1, D_MODEL)),
            _const_spec((1, D_MODEL)),
            _const_spec((1, D_MODEL)),
            _layer_spec((D_MODEL, D_MODEL), 0, 0, 0),
            _const_spec((1, D_MODEL)),
            _mod_spec(layer, False, 2),
        ],
        out_specs=pl.BlockSpec((ROWS, D_MODEL), lambda i: (i, 0)),
        out_shape=jax.ShapeDtypeStruct((n, D_MODEL), F32),
        scratch_shapes=[
            pltpu.VMEM((ROWS + 2 * CONF_HALO, D_MODEL), F32),
            pltpu.VMEM((ROWS, D_MODEL), F32),
        ],
        compiler_params=_cparams(),
        name="conf_out",
    )(z, z, z, x, conv_w, conv_b, ln_g, ln_b, w_pw2, b_pw2, mods)


def _pos_tables(rows, d):
    q = d // 4
    omega = 1.0 / (POS_BASE ** (jnp.arange(q, dtype=F32) / q))
    er = jnp.arange(rows, dtype=jnp.int32).astype(F32)[:, None] * omega[None, :]
    ec = jnp.arange(GRID_W, dtype=jnp.int32).astype(F32)[:, None] * omega[None, :]
    rowtab = jnp.concatenate([jnp.sin(er), jnp.cos(er)], axis=-1)
    coltab = jnp.concatenate([jnp.sin(ec), jnp.cos(ec)], axis=-1)
    return rowtab.reshape(rows, 1, d // 2), coltab.reshape(GRID_W, 1, d // 2)


def _block_diag_windows(w_a, w_x):
    eye = 0.5 * jnp.eye(N_RNN_BLOCKS, dtype=F32)

    def dense(w):
        return (w[:, :, :, None, :] * eye[None, :, None, :, None]).reshape(2, D_RNN, D_RNN)

    da, dx = dense(w_a), dense(w_x)
    wins = []
    for j, k0 in enumerate(BD_K0):
        cols = slice(j * MXU_N, (j + 1) * MXU_N)
        wins.append(jnp.concatenate([da[:, k0:k0 + BD_WIN, cols], dx[:, k0:k0 + BD_WIN, cols]], axis=-1))
    return jnp.stack(wins, axis=1).astype(BF16)


def kernel(x, c, ctx, c_ctx, w_ada, b_ada, norm_g, rec_w_in, rec_conv_w, rec_conv_b, rec_lambda, rec_w_a, rec_b_a,
           rec_w_x, rec_b_x, rec_w_out, conf_w_pw1, conf_b_pw1, conf_conv_w, conf_conv_b, conf_ln_g, conf_ln_b,
           conf_w_pw2, conf_b_pw2, mlp_w_in, mlp_w_out, final_g):
    assert x.shape == (BATCH, SEQ, D_MODEL) and ctx.shape == (BATCH, CTX_LEN, D_MODEL)
    assert w_ada.shape[0] == 2, "one recurrent layer followed by one conformer layer"

    cc = jnp.concatenate([c, jnp.broadcast_to(c_ctx[None, :], (BATCH, D_MODEL))], axis=0)
    mods = _ada_table(cc, w_ada, b_ada)
    norm_g = norm_g.reshape(4, 1, D_MODEL)
    mlp_w1, mlp_w2 = mlp_w_in.astype(BF16), mlp_w_out.astype(BF16)

    w_in = rec_w_in.astype(BF16)
    conv_b = rec_conv_b.reshape(1, 1, D_RNN)
    wab = _block_diag_windows(rec_w_a[0], rec_w_x[0])
    bab = 0.5 * jnp.stack([rec_b_a[0].reshape(2, D_RNN), rec_b_x[0].reshape(2, D_RNN)], axis=1)
    lam = rec_lambda[0].reshape(2, 1, D_RNN)
    zeros = jnp.zeros((BATCH, D_RNN), F32)
    rec = (lam, wab, bab)

    u_ctx, h0_f = _rec_project(ctx, None, mods, True, norm_g, w_in, rec_conv_w, conv_b, zeros, *rec, False)
    (h0_r,) = _rec_scan_rev(u_ctx, zeros, *rec, mode="state")

    pos_tabs = _pos_tables(SEQ // GRID_W, D_MODEL)
    x0, u_lat, gate, y_f, _ = _rec_project(x, pos_tabs, mods, False, norm_g, w_in, rec_conv_w, conv_b, h0_f, *rec,
                                           True)
    x1, _ = _rec_scan_rev(u_lat, h0_r, *rec, mode="out", extra=(y_f, gate, x0, rec_w_out.astype(BF16)), mods=mods)
    x2 = _mlp(x1, mods, 0, norm_g, mlp_w1, mlp_w2)

    z = _conf_in(x2, mods, 1, norm_g, conf_w_pw1.astype(BF16), conf_b_pw1.reshape(1, 2 * D_MODEL))
    x3 = _conf_out(z, x2, mods, 1, conf_conv_w, conf_conv_b.reshape(1, D_MODEL),
                   conf_ln_g.reshape(1, D_MODEL), conf_ln_b.reshape(1, D_MODEL),
                   conf_w_pw2.astype(BF16), conf_b_pw2.reshape(1, D_MODEL))
    return _mlp(x3, mods, 1, norm_g, mlp_w1, mlp_w2, final_g=final_g.reshape(1, D_MODEL))
```

```python
import functools

import jax
import jax.numpy as jnp
import numpy as np
from jax import lax
from jax.experimental import pallas as pl
from jax.experimental.pallas import tpu as pltpu

F32 = jnp.float32
BF16 = jnp.bfloat16

D_MODEL = 1024
BATCH = 8
SEQ = 2048
CTX_LEN = 256
GRID_W = 64
D_RNN = 1280
N_RNN_BLOCKS = 16
RNN_BLOCK = D_RNN // N_RNN_BLOCKS
REC_CONV_W = 4
RG_C = 8.0
CONF_KW = 31
CONF_HALF = CONF_KW // 2
D_FF = 4 * D_MODEL
N_MOD = 6
EPS = 1e-6
POS_BASE = 10000.0

VMEM_LIMIT_BYTES = 56 * 1024 * 1024
LANES = 128
MXU_N = 256
BD_WIN = 512
BD_K0 = (0, 128, 384, 640, 768)
N_BD = D_RNN // MXU_N

TT = 64
ROWS = TT * BATCH
HALO = 16
HALO_T = HALO // BATCH
XHALO_T = 8
CONF_HALO = 128


def _cparams():
    return pltpu.CompilerParams(dimension_semantics=("arbitrary",), vmem_limit_bytes=VMEM_LIMIT_BYTES)


def _const_spec(shape):
    return pl.BlockSpec(shape, lambda *_: (0,) * len(shape), pipeline_mode=pl.Buffered(1))


def _layer_spec(shape, *idx):
    return pl.BlockSpec((None,) + tuple(shape), lambda *_: tuple(idx), pipeline_mode=pl.Buffered(1))


def _mod_spec(layer, ctx, k):
    return _layer_spec((BATCH, D_MODEL), layer, int(ctx), k)


def _sigmoid(x):
    return 0.5 * jnp.tanh(0.5 * x) + 0.5


def _rms(x3, g):
    ms = jnp.mean(x3 * x3, axis=-1, keepdims=True)
    return x3 * lax.rsqrt(ms + EPS) * g


def _norm_mod(x3, g, sh, sc):
    return _rms(x3, g) * (1.0 + sc) + sh


def _ada_kernel(c_ref, w_ref, b_ref, o_ref):
    c = c_ref[...]
    s = (c * jax.nn.sigmoid(c)).astype(BF16)
    o_ref[...] = jnp.dot(s, w_ref[...].astype(BF16), preferred_element_type=F32) + b_ref[...]


def _ada_table(cc, w_ada, b_ada):
    depth = w_ada.shape[0]
    tn = 1536
    return pl.pallas_call(
        _ada_kernel,
        grid=(depth, N_MOD * D_MODEL // tn),
        in_specs=[
            pl.BlockSpec((2 * BATCH, D_MODEL), lambda l, j: (0, 0)),
            pl.BlockSpec((None, D_MODEL, tn), lambda l, j: (l, 0, j)),
            pl.BlockSpec((None, 1, tn), lambda l, j: (l, 0, j)),
        ],
        out_specs=pl.BlockSpec((None, 2 * BATCH, tn), lambda l, j: (l, 0, j)),
        out_shape=jax.ShapeDtypeStruct((depth, 2 * BATCH, N_MOD * D_MODEL), F32),
        compiler_params=pltpu.CompilerParams(
            dimension_semantics=("arbitrary", "arbitrary"), vmem_limit_bytes=VMEM_LIMIT_BYTES),
        name="ada_table",
    )(cc, w_ada, b_ada.reshape(depth, 1, N_MOD * D_MODEL))


def _fir(src_ref, dst_ref, w_ref, b_ref, *, n_taps, src_off, n_out, width, block, unrolled):
    for c in range(width // LANES):
        lanes = slice(c * LANES, (c + 1) * LANES)
        wk = [jnp.broadcast_to(w_ref[k:k + 1, lanes], (BATCH, LANES)) for k in range(n_taps)]
        bias = jnp.broadcast_to(b_ref[:, lanes], (BATCH, LANES))

        def one_block(base):
            acc = [bias] * block
            for q in range(block + n_taps - 1):
                z = src_ref[pl.ds(base + src_off + q * BATCH, BATCH), lanes]
                for j in range(max(0, q - n_taps + 1), min(block, q + 1)):
                    acc[j] = acc[j] + wk[q - j] * z
            for j in range(block):
                dst_ref[pl.ds(base + j * BATCH, BATCH), lanes] = acc[j]

        if unrolled:
            for blk in range(n_out // block):
                one_block(blk * block * BATCH)
        else:
            def body(blk, carry):
                one_block(pl.multiple_of(blk * (block * BATCH), block * BATCH))
                return carry

            lax.fori_loop(0, n_out // block, body, 0)


SCAN_RB = 128
LOG2_E = 1.4426950408889634


def _rglru_coeffs(u_ref, ub_scr, lam_ref, wab_ref, bab_ref, a_scr, b_scr):
    lam = lam_ref[...]
    softplus = jnp.maximum(-lam, 0.0) + jnp.log1p(jnp.exp(-jnp.abs(lam)))
    k = (-0.5 * RG_C * LOG2_E) * softplus
    ub_scr[...] = u_ref[...].astype(BF16)
    for rb in range(ROWS // SCAN_RB):
        rows = slice(rb * SCAN_RB, (rb + 1) * SCAN_RB)
        for j in range(N_BD):
            cols = slice(j * MXU_N, (j + 1) * MXU_N)
            pre = jnp.dot(ub_scr[rows, BD_K0[j]:BD_K0[j] + BD_WIN], wab_ref[j],
                          preferred_element_type=F32)
            t_r = jnp.tanh(pre[:, :MXU_N] + bab_ref[0:1, cols])
            t_i = jnp.tanh(pre[:, MXU_N:] + bab_ref[1:2, cols])
            a = jnp.exp2(k[:, cols] * t_r + k[:, cols])
            w = 1.0 - a * a
            sq = jnp.where(w > 0.0, w * lax.rsqrt(w), 0.0)
            a_scr[rows, cols] = a
            b_scr[rows, cols] = (0.5 * sq) * ((t_i + 1.0) * u_ref[rows, cols])


def _rglru_scan(a_scr, b_scr, h_scr, reverse):
    def step(s, h):
        t = (TT - 1 - s) if reverse else s
        r8 = pl.ds(pl.multiple_of(t * BATCH, BATCH), BATCH)
        h = a_scr[r8, :] * h + b_scr[r8, :]
        b_scr[r8, :] = h
        return h

    h = lax.fori_loop(0, TT, step, h_scr[...], unroll=8)
    h_scr[...] = h
    return h


def _proj_kernel(*refs, nchunks, add_pos, want_gate):
    it = iter(refs)
    xm_ref, xp_ref, xn_ref = next(it), next(it), next(it)
    if add_pos:
        rowtab_ref, coltab_ref = next(it), next(it)
    g_ref, sh_ref, sc_ref = next(it), next(it), next(it)
    wr_ref, cw_ref, cb_ref = next(it), next(it), next(it)
    h0_ref, lam_ref, wab_ref, bab_ref = next(it), next(it), next(it), next(it)
    if want_gate:
        wg_ref = next(it)
    if add_pos:
        x0_ref = next(it)
    u_ref = next(it)
    if want_gate:
        gate_ref = next(it)
        yf_ref = next(it)
    hT_ref = next(it)
    zr_scr, ub_scr, a_scr, b_scr, h_scr = next(it), next(it), next(it), next(it), next(it)

    i = pl.program_id(0)

    @pl.when(i == 0)
    def _():
        h_scr[...] = h0_ref[...]

    xm = jnp.swapaxes(xm_ref[...], 0, 1)
    xp = jnp.swapaxes(xp_ref[...], 0, 1)[XHALO_T - HALO_T:]
    xn = jnp.swapaxes(xn_ref[...], 0, 1)[:HALO_T]
    x3 = jnp.concatenate([xp, xm, xn], axis=0)
    if add_pos:
        half = D_MODEL // 2
        r_prev = rowtab_ref[pl.ds(jnp.maximum(i - 1, 0), 1)]
        r_here = rowtab_ref[pl.ds(i, 1)]
        r_next = rowtab_ref[pl.ds(jnp.minimum(i + 1, nchunks - 1), 1)]
        prow = jnp.concatenate([jnp.broadcast_to(r_prev, (HALO_T, 1, half)),
                                jnp.broadcast_to(r_here, (TT, 1, half)),
                                jnp.broadcast_to(r_next, (HALO_T, 1, half))], axis=0)
        pcol = jnp.concatenate([coltab_ref[GRID_W - HALO_T:], coltab_ref[...], coltab_ref[:HALO_T]], axis=0)
        x3 = x3 + jnp.concatenate([prow, pcol], axis=-1)
        x0_ref[...] = x3[HALO_T:HALO_T + TT].reshape(ROWS, D_MODEL)
    h = _norm_mod(x3, g_ref[...], sh_ref[...], sc_ref[...])
    h = h.reshape(ROWS + 2 * HALO, D_MODEL).astype(BF16)

    zr_scr[...] = jnp.dot(h, wr_ref[...], preferred_element_type=F32)

    @pl.when(i == 0)
    def _():
        zr_scr[0:HALO, :] = jnp.zeros((HALO, D_RNN), F32)

    @pl.when(i == nchunks - 1)
    def _():
        zr_scr[ROWS + HALO:ROWS + 2 * HALO, :] = jnp.zeros((HALO, D_RNN), F32)

    _fir(zr_scr, u_ref, cw_ref, cb_ref, n_taps=REC_CONV_W, src_off=HALO - BATCH, n_out=TT, width=D_RNN, block=16,
         unrolled=True)

    if want_gate:
        zg = jnp.dot(h[HALO:HALO + ROWS], wg_ref[...], preferred_element_type=F32)
        gate_ref[...] = jax.nn.gelu(zg).astype(BF16)

    _rglru_coeffs(u_ref, ub_scr, lam_ref, wab_ref, bab_ref, a_scr, b_scr)
    hT_ref[...] = _rglru_scan(a_scr, b_scr, h_scr, reverse=False)
    if want_gate:
        yf_ref[...] = b_scr[...]


def _rec_project(x, pos_tabs, mods, ctx, norm_g, w_in, conv_w, conv_b, h0, lam, wab, bab, want_gate):
    t_len = x.shape[1]
    n = t_len * BATCH
    nchunks = t_len // TT
    add_pos = pos_tabs is not None
    hb = TT // XHALO_T
    nhb = t_len // XHALO_T

    in_specs = [
        pl.BlockSpec((BATCH, TT, D_MODEL), lambda i: (0, i, 0)),
        pl.BlockSpec((BATCH, XHALO_T, D_MODEL), lambda i: (0, jnp.maximum(i * hb - 1, 0), 0)),
        pl.BlockSpec((BATCH, XHALO_T, D_MODEL), lambda i: (0, jnp.minimum((i + 1) * hb, nhb - 1), 0)),
    ]
    args = [x, x, x]
    if add_pos:
        assert TT == GRID_W and pos_tabs[0].shape[0] == nchunks
        in_specs += [_const_spec(pos_tabs[0].shape), _const_spec(pos_tabs[1].shape)]
        args += list(pos_tabs)
    in_specs += [
        _layer_spec((1, D_MODEL), 0, 0, 0),
        _mod_spec(0, ctx, 0),
        _mod_spec(0, ctx, 1),
        _layer_spec((D_MODEL, D_RNN), 0, 0, 1),
        _layer_spec((REC_CONV_W, D_RNN), 0, 0, 0),
        _layer_spec((1, D_RNN), 0, 0, 0),
    ]
    args += [norm_g, mods, mods, w_in, conv_w, conv_b]
    in_specs += [
        _const_spec((BATCH, D_RNN)),
        _layer_spec((1, D_RNN), 0, 0, 0),
        _layer_spec((N_BD, BD_WIN, 2 * MXU_N), 0, 0, 0, 0),
        _layer_spec((2, D_RNN), 0, 0, 0),
    ]
    args += [h0, lam, wab, bab]
    if want_gate:
        in_specs.append(_layer_spec((D_MODEL, D_RNN), 0, 0, 0))
        args.append(w_in)

    out_specs, out_shape = [], []
    if add_pos:
        out_specs.append(pl.BlockSpec((ROWS, D_MODEL), lambda i: (i, 0)))
        out_shape.append(jax.ShapeDtypeStruct((n, D_MODEL), F32))
    out_specs.append(pl.BlockSpec((ROWS, D_RNN), lambda i: (i, 0)))
    out_shape.append(jax.ShapeDtypeStruct((n, D_RNN), F32))
    if want_gate:
        out_specs.append(pl.BlockSpec((ROWS, D_RNN), lambda i: (i, 0)))
        out_shape.append(jax.ShapeDtypeStruct((n, D_RNN), BF16))
        out_specs.append(pl.BlockSpec((ROWS, D_RNN), lambda i: (i, 0)))
        out_shape.append(jax.ShapeDtypeStruct((n, D_RNN), F32))
    out_specs.append(pl.BlockSpec((BATCH, D_RNN), lambda i: (0, 0)))
    out_shape.append(jax.ShapeDtypeStruct((BATCH, D_RNN), F32))

    return pl.pallas_call(
        functools.partial(_proj_kernel, nchunks=nchunks, add_pos=add_pos, want_gate=want_gate),
        grid=(nchunks,),
        in_specs=in_specs,
        out_specs=out_specs,
        out_shape=out_shape,
        scratch_shapes=[
            pltpu.VMEM((ROWS + 2 * HALO, D_RNN), F32),
            pltpu.VMEM((ROWS, D_RNN), BF16),
            pltpu.VMEM((ROWS, D_RNN), F32),
            pltpu.VMEM((ROWS, D_RNN), F32),
            pltpu.VMEM((BATCH, D_RNN), F32),
        ],
        compiler_params=_cparams(),
        name="rec_fwd_lat" if add_pos else "rec_fwd_ctx",
    )(*args)


def _scan_kernel(*refs, mode):
    it = iter(refs)
    u_ref, h0_ref, lam_ref, wab_ref, bab_ref = next(it), next(it), next(it), next(it), next(it)
    if mode == "out":
        yf_ref, gate_ref, x0_ref, g1_ref, wo_ref = next(it), next(it), next(it), next(it), next(it)
        x1_ref = next(it)
    hT_ref = next(it)
    ub_scr, a_scr, b_scr, h_scr = next(it), next(it), next(it), next(it)

    i = pl.program_id(0)

    @pl.when(i == 0)
    def _():
        h_scr[...] = h0_ref[...]

    _rglru_coeffs(u_ref, ub_scr, lam_ref, wab_ref, bab_ref, a_scr, b_scr)
    hT_ref[...] = _rglru_scan(a_scr, b_scr, h_scr, reverse=True)

    if mode == "out":
        y = b_scr[...] + yf_ref[...]
        gy = (gate_ref[...].astype(F32) * y).astype(BF16)
        o = jnp.dot(gy, wo_ref[...], preferred_element_type=F32)
        x1 = x0_ref[...].reshape(TT, BATCH, D_MODEL) + g1_ref[...] * o.reshape(TT, BATCH, D_MODEL)
        x1_ref[...] = x1.reshape(ROWS, D_MODEL)


def _rec_scan_rev(u, h0, lam, wab, bab, *, mode, extra=None, mods=None):
    n = u.shape[0]
    nchunks = n // ROWS
    order = lambda i: (nchunks - 1 - i, 0)
    in_specs = [
        pl.BlockSpec((ROWS, D_RNN), order),
        _const_spec((BATCH, D_RNN)),
        _layer_spec((1, D_RNN), 1, 0, 0),
        _layer_spec((N_BD, BD_WIN, 2 * MXU_N), 1, 0, 0, 0),
        _layer_spec((2, D_RNN), 1, 0, 0),
    ]
    args = [u, h0, lam, wab, bab]
    out_specs, out_shape = [], []
    if mode == "out":
        yf, gate, x0, w_out = extra
        in_specs += [
            pl.BlockSpec((ROWS, D_RNN), order),
            pl.BlockSpec((ROWS, D_RNN), order),
            pl.BlockSpec((ROWS, D_MODEL), order),
            _mod_spec(0, False, 2),
            _layer_spec((D_RNN, D_MODEL), 0, 0, 0),
        ]
        args += [yf, gate, x0, mods, w_out]
        out_specs.append(pl.BlockSpec((ROWS, D_MODEL), order))
        out_shape.append(jax.ShapeDtypeStruct((n, D_MODEL), F32))
    out_specs.append(pl.BlockSpec((BATCH, D_RNN), lambda i: (0, 0)))
    out_shape.append(jax.ShapeDtypeStruct((BATCH, D_RNN), F32))

    return pl.pallas_call(
        functools.partial(_scan_kernel, mode=mode),
        grid=(nchunks,),
        in_specs=in_specs,
        out_specs=out_specs,
        out_shape=out_shape,
        scratch_shapes=[
            pltpu.VMEM((ROWS, D_RNN), BF16),
            pltpu.VMEM((ROWS, D_RNN), F32),
            pltpu.VMEM((ROWS, D_RNN), F32),
            pltpu.VMEM((BATCH, D_RNN), F32),
        ],
        compiler_params=_cparams(),
        name="rec_rev_" + mode,
    )(*args)


def _mlp_kernel(*refs, final):
    it = iter(refs)
    x_ref, g_ref, sh_ref, sc_ref, gt_ref, w1_ref, w2_ref = (next(it) for _ in range(7))
    if final:
        fg_ref = next(it)
    o_ref = next(it)

    x3 = x_ref[...].reshape(TT, BATCH, D_MODEL)
    h = _norm_mod(x3, g_ref[...], sh_ref[...], sc_ref[...]).reshape(ROWS, D_MODEL).astype(BF16)
    m = jnp.dot(h, w1_ref[...], preferred_element_type=F32)
    m = jnp.square(jnp.maximum(m, 0.0)).astype(BF16)
    o = jnp.dot(m, w2_ref[...], preferred_element_type=F32)
    y = x3 + gt_ref[...] * o.reshape(TT, BATCH, D_MODEL)
    if final:
        o_ref[...] = jnp.swapaxes(_rms(y, fg_ref[...]), 0, 1)
    else:
        o_ref[...] = y.reshape(ROWS, D_MODEL)


def _mlp(x, mods, layer, norm_g, w1, w2, final_g=None):
    n = x.shape[0]
    final = final_g is not None
    in_specs = [
        pl.BlockSpec((ROWS, D_MODEL), lambda i: (i, 0)),
        _layer_spec((1, D_MODEL), 2 * layer + 1, 0, 0),
        _mod_spec(layer, False, 3),
        _mod_spec(layer, False, 4),
        _mod_spec(layer, False, 5),
        _layer_spec((D_MODEL, D_FF), layer, 0, 0),
        _layer_spec((D_FF, D_MODEL), layer, 0, 0),
    ]
    args = [x, norm_g, mods, mods, mods, w1, w2]
    if final:
        in_specs.append(_const_spec((1, D_MODEL)))
        args.append(final_g)
        out_spec = pl.BlockSpec((BATCH, TT, D_MODEL), lambda i: (0, i, 0))
        out_shape = jax.ShapeDtypeStruct((BATCH, n // BATCH, D_MODEL), F32)
    else:
        out_spec = pl.BlockSpec((ROWS, D_MODEL), lambda i: (i, 0))
        out_shape = jax.ShapeDtypeStruct((n, D_MODEL), F32)
    return pl.pallas_call(
        functools.partial(_mlp_kernel, final=final),
        grid=(n // ROWS,),
        in_specs=in_specs,
        out_specs=out_spec,
        out_shape=out_shape,
        compiler_params=_cparams(),
        name="mlp_final" if final else "mlp",
    )(*args)


def _conf_in_kernel(x_ref, g_ref, sh_ref, sc_ref, w_ref, b_ref, z_ref):
    x3 = x_ref[...].reshape(TT, BATCH, D_MODEL)
    h = _norm_mod(x3, g_ref[...], sh_ref[...], sc_ref[...]).reshape(ROWS, D_MODEL).astype(BF16)
    z = jnp.dot(h, w_ref[...], preferred_element_type=F32) + b_ref[...]
    z_ref[...] = (z[:, :D_MODEL] * _sigmoid(z[:, D_MODEL:])).astype(BF16)


def _conf_in(x, mods, layer, norm_g, w_pw1, b_pw1):
    n = x.shape[0]
    return pl.pallas_call(
        _conf_in_kernel,
        grid=(n // ROWS,),
        in_specs=[
            pl.BlockSpec((ROWS, D_MODEL), lambda i: (i, 0)),
            _layer_spec((1, D_MODEL), 2 * layer, 0, 0),
            _mod_spec(layer, False, 0),
            _mod_spec(layer, False, 1),
            _layer_spec((D_MODEL, 2 * D_MODEL), 0, 0, 0),
            _const_spec((1, 2 * D_MODEL)),
        ],
        out_specs=pl.BlockSpec((ROWS, D_MODEL), lambda i: (i, 0)),
        out_shape=jax.ShapeDtypeStruct((n, D_MODEL), BF16),
        compiler_params=_cparams(),
        name="conf_in",
    )(x, norm_g, mods, mods, w_pw1, b_pw1)


def _conf_out_kernel(zm_ref, zp_ref, zn_ref, x_ref, cw_ref, cb_ref, lg_ref, lb_ref, w_ref, b_ref, g1_ref,
                     o_ref, zext_scr, conv_scr, *, nchunks):
    i = pl.program_id(0)
    zext_scr[0:CONF_HALO, :] = zp_ref[...].astype(F32)
    zext_scr[CONF_HALO:CONF_HALO + ROWS, :] = zm_ref[...].astype(F32)
    zext_scr[CONF_HALO + ROWS:, :] = zn_ref[...].astype(F32)

    @pl.when(i == 0)
    def _():
        zext_scr[0:CONF_HALO, :] = jnp.zeros((CONF_HALO, D_MODEL), F32)

    @pl.when(i == nchunks - 1)
    def _():
        zext_scr[CONF_HALO + ROWS:, :] = jnp.zeros((CONF_HALO, D_MODEL), F32)

    _fir(zext_scr, conv_scr, cw_ref, cb_ref, n_taps=CONF_KW, src_off=CONF_HALO - CONF_HALF * BATCH,
         n_out=TT, width=D_MODEL, block=16, unrolled=False)

    z = conv_scr[...]
    mu = jnp.mean(z, axis=-1, keepdims=True)
    zc = z - mu
    var = jnp.mean(zc * zc, axis=-1, keepdims=True)
    zl = zc * lax.rsqrt(var + EPS) * lg_ref[...] + lb_ref[...]
    zs = (zl * _sigmoid(zl)).astype(BF16)
    y = jnp.dot(zs, w_ref[...], preferred_element_type=F32) + b_ref[...]
    x1 = x_ref[...].reshape(TT, BATCH, D_MODEL) + g1_ref[...] * y.reshape(TT, BATCH, D_MODEL)
    o_ref[...] = x1.reshape(ROWS, D_MODEL)


def _conf_out(z, x, mods, layer, conv_w, conv_b, ln_g, ln_b, w_pw2, b_pw2):
    n = x.shape[0]
    nchunks = n // ROWS
    hb = ROWS // CONF_HALO
    return pl.pallas_call(
        functools.partial(_conf_out_kernel, nchunks=nchunks),
        grid=(nchunks,),
        in_specs=[
            pl.BlockSpec((ROWS, D_MODEL), lambda i: (i, 0)),
            pl.BlockSpec((CONF_HALO, D_MODEL), lambda i: (jnp.maximum(i * hb - 1, 0), 0)),
            pl.BlockSpec((CONF_HALO, D_MODEL), lambda i: (jnp.minimum((i + 1) * hb, n // CONF_HALO - 1), 0)),
            pl.BlockSpec((ROWS, D_MODEL), lambda i: (i, 0)),
            _layer_spec((CONF_KW, D_MODEL), 0, 0, 0),
            _const_spec((1, D_MODEL)),
            _const_spec((1, D_MODEL)),
            _const_spec((1, D_MODEL)),
            _layer_spec((D_MODEL, D_MODEL), 0, 0, 0),
            _const_spec((1, D_MODEL)),
            _mod_spec(layer, False, 2),
        ],
        out_specs=pl.BlockSpec((ROWS, D_MODEL), lambda i: (i, 0)),
        out_shape=jax.ShapeDtypeStruct((n, D_MODEL), F32),
        scratch_shapes=[
            pltpu.VMEM((ROWS + 2 * CONF_HALO, D_MODEL), F32),
            pltpu.VMEM((ROWS, D_MODEL), F32),
        ],
        compiler_params=_cparams(),
        name="conf_out",
    )(z, z, z, x, conv_w, conv_b, ln_g, ln_b, w_pw2, b_pw2, mods)


def _pos_tables(rows, d):
    q = d // 4
    omega = 1.0 / (POS_BASE ** (jnp.arange(q, dtype=F32) / q))
    er = jnp.arange(rows, dtype=jnp.int32).astype(F32)[:, None] * omega[None, :]
    ec = jnp.arange(GRID_W, dtype=jnp.int32).astype(F32)[:, None] * omega[None, :]
    rowtab = jnp.concatenate([jnp.sin(er), jnp.cos(er)], axis=-1)
    coltab = jnp.concatenate([jnp.sin(ec), jnp.cos(ec)], axis=-1)
    return rowtab.reshape(rows, 1, d // 2), coltab.reshape(GRID_W, 1, d // 2)


def _block_diag_windows(w_a, w_x):
    w = jnp.stack([w_a, w_x], axis=1).reshape(2, 2, D_RNN, RNN_BLOCK)
    tiled_eye = np.tile(np.eye(RNN_BLOCK, dtype=np.float32), (1, N_RNN_BLOCKS))
    row_blk = np.arange(D_RNN)[:, None] // RNN_BLOCK
    col_blk = np.arange(D_RNN)[None, :] // RNN_BLOCK
    w_win, sel, msk = [], [], []
    for j, k0 in enumerate(BD_K0):
        cols = slice(j * MXU_N, (j + 1) * MXU_N)
        w_win.append(w[:, :, k0:k0 + BD_WIN])
        sel.append(tiled_eye[:, cols])
        msk.append(0.5 * (row_blk[k0:k0 + BD_WIN] == col_blk[:, cols]))
    w_win = jnp.stack(w_win, axis=2)
    sel = jnp.asarray(np.stack(sel))
    msk = jnp.asarray(np.stack(msk), dtype=F32)
    full = jnp.einsum("dgjrk,jkc->djrgc", w_win, sel) * msk[None, :, :, None, :]
    return full.reshape(2, N_BD, BD_WIN, 2 * MXU_N).astype(BF16)


def kernel(x, c, ctx, c_ctx, w_ada, b_ada, norm_g, rec_w_in, rec_conv_w, rec_conv_b, rec_lambda, rec_w_a, rec_b_a,
           rec_w_x, rec_b_x, rec_w_out, conf_w_pw1, conf_b_pw1, conf_conv_w, conf_conv_b, conf_ln_g, conf_ln_b,
           conf_w_pw2, conf_b_pw2, mlp_w_in, mlp_w_out, final_g):
    assert x.shape == (BATCH, SEQ, D_MODEL) and ctx.shape == (BATCH, CTX_LEN, D_MODEL)
    assert w_ada.shape[0] == 2, "one recurrent layer followed by one conformer layer"

    cc = jnp.concatenate([c, jnp.broadcast_to(c_ctx[None, :], (BATCH, D_MODEL))], axis=0)
    mods = _ada_table(cc, w_ada, b_ada)
    norm_g = norm_g.reshape(4, 1, D_MODEL)
    mlp_w1, mlp_w2 = mlp_w_in.astype(BF16), mlp_w_out.astype(BF16)

    w_in = rec_w_in.astype(BF16)
    conv_b = rec_conv_b.reshape(1, 1, D_RNN)
    wab = _block_diag_windows(rec_w_a[0], rec_w_x[0])
    bab = 0.5 * jnp.stack([rec_b_a[0].reshape(2, D_RNN), rec_b_x[0].reshape(2, D_RNN)], axis=1)
    lam = rec_lambda[0].reshape(2, 1, D_RNN)
    zeros = jnp.zeros((BATCH, D_RNN), F32)
    rec = (lam, wab, bab)

    u_ctx, h0_f = _rec_project(ctx, None, mods, True, norm_g, w_in, rec_conv_w, conv_b, zeros, *rec, False)
    (h0_r,) = _rec_scan_rev(u_ctx, zeros, *rec, mode="state")

    pos_tabs = _pos_tables(SEQ // GRID_W, D_MODEL)
    x0, u_lat, gate, y_f, _ = _rec_project(x, pos_tabs, mods, False, norm_g, w_in, rec_conv_w, conv_b, h0_f, *rec,
                                           True)
    x1, _ = _rec_scan_rev(u_lat, h0_r, *rec, mode="out", extra=(y_f, gate, x0, rec_w_out.astype(BF16)), mods=mods)
    x2 = _mlp(x1, mods, 0, norm_g, mlp_w1, mlp_w2)

    z = _conf_in(x2, mods, 1, norm_g, conf_w_pw1.astype(BF16), conf_b_pw1.reshape(1, 2 * D_MODEL))
    x3 = _conf_out(z, x2, mods, 1, conf_conv_w, conf_conv_b.reshape(1, D_MODEL),
                   conf_ln_g.reshape(1, D_MODEL), conf_ln_b.reshape(1, D_MODEL),
                   conf_w_pw2.astype(BF16), conf_b_pw2.reshape(1, D_MODEL))
    return _mlp(x3, mods, 1, norm_g, mlp_w1, mlp_w2, final_g=final_g.reshape(1, D_MODEL))
```

```python
import functools

import jax
import jax.numpy as jnp
import numpy as np
from jax import lax
from jax.experimental import pallas as pl
from jax.experimental.pallas import tpu as pltpu

F32 = jnp.float32
BF16 = jnp.bfloat16

D_MODEL = 1024
BATCH = 8
SEQ = 2048
CTX_LEN = 256
GRID_W = 64
D_RNN = 1280
N_RNN_BLOCKS = 16
RNN_BLOCK = D_RNN // N_RNN_BLOCKS
REC_CONV_W = 4
RG_C = 8.0
CONF_KW = 31
CONF_HALF = CONF_KW // 2
D_FF = 4 * D_MODEL
N_MOD = 6
EPS = 1e-6
POS_BASE = 10000.0

VMEM_LIMIT_BYTES = 56 * 1024 * 1024
LANES = 128
MXU_N = 256
BD_WIN = 512
BD_K0 = (0, 128, 384, 640, 768)
N_BD = D_RNN // MXU_N

TT = 64
ROWS = TT * BATCH
HALO = 16
HALO_T = HALO // BATCH
XHALO_T = 8
CONF_HALO = 128


def _cparams():
    return pltpu.CompilerParams(dimension_semantics=("arbitrary",), vmem_limit_bytes=VMEM_LIMIT_BYTES)


def _const_spec(shape):
    return pl.BlockSpec(shape, lambda *_: (0,) * len(shape), pipeline_mode=pl.Buffered(1))


def _layer_spec(shape, *idx):
    return pl.BlockSpec((None,) + tuple(shape), lambda *_: tuple(idx), pipeline_mode=pl.Buffered(1))


def _mod_spec(layer, ctx, k):
    return _layer_spec((BATCH, D_MODEL), layer, int(ctx), k)


def _sigmoid(x):
    return 0.5 * jnp.tanh(0.5 * x) + 0.5


def _rms(x3, g):
    ms = jnp.mean(x3 * x3, axis=-1, keepdims=True)
    return x3 * lax.rsqrt(ms + EPS) * g


def _norm_mod(x3, g, sh, sc):
    return _rms(x3, g) * (1.0 + sc) + sh


def _ada_kernel(c_ref, w_ref, b_ref, o_ref):
    c = c_ref[...]
    s = (c * jax.nn.sigmoid(c)).astype(BF16)
    o_ref[...] = jnp.dot(s, w_ref[...].astype(BF16), preferred_element_type=F32) + b_ref[...]


def _ada_table(cc, w_ada, b_ada):
    depth = w_ada.shape[0]
    tn = 1536
    return pl.pallas_call(
        _ada_kernel,
        grid=(depth, N_MOD * D_MODEL // tn),
        in_specs=[
            pl.BlockSpec((2 * BATCH, D_MODEL), lambda l, j: (0, 0)),
            pl.BlockSpec((None, D_MODEL, tn), lambda l, j: (l, 0, j)),
            pl.BlockSpec((None, 1, tn), lambda l, j: (l, 0, j)),
        ],
        out_specs=pl.BlockSpec((None, 2 * BATCH, tn), lambda l, j: (l, 0, j)),
        out_shape=jax.ShapeDtypeStruct((depth, 2 * BATCH, N_MOD * D_MODEL), F32),
        compiler_params=pltpu.CompilerParams(
            dimension_semantics=("arbitrary", "arbitrary"), vmem_limit_bytes=VMEM_LIMIT_BYTES),
        name="ada_table",
    )(cc, w_ada, b_ada.reshape(depth, 1, N_MOD * D_MODEL))


def _fir(src_ref, dst_ref, w_ref, b_ref, *, n_taps, src_off, n_out, width, block, unrolled):
    for c in range(width // LANES):
        lanes = slice(c * LANES, (c + 1) * LANES)
        wk = [jnp.broadcast_to(w_ref[k:k + 1, lanes], (BATCH, LANES)) for k in range(n_taps)]
        bias = jnp.broadcast_to(b_ref[:, lanes], (BATCH, LANES))

        def one_block(base):
            acc = [bias] * block
            for q in range(block + n_taps - 1):
                z = src_ref[pl.ds(base + src_off + q * BATCH, BATCH), lanes]
                for j in range(max(0, q - n_taps + 1), min(block, q + 1)):
                    acc[j] = acc[j] + wk[q - j] * z
            for j in range(block):
                dst_ref[pl.ds(base + j * BATCH, BATCH), lanes] = acc[j]

        if unrolled:
            for blk in range(n_out // block):
                one_block(blk * block * BATCH)
        else:
            def body(blk, carry):
                one_block(pl.multiple_of(blk * (block * BATCH), block * BATCH))
                return carry

            lax.fori_loop(0, n_out // block, body, 0, unroll=2)


SCAN_RB = 128
LOG2_E = 1.4426950408889634


def _rglru(u_ref, ub_scr, lam_ref, wab_ref, bab_ref, a_scr, b_scr, h_scr, reverse):
    lam = lam_ref[...]
    softplus = jnp.maximum(-lam, 0.0) + jnp.log1p(jnp.exp(-jnp.abs(lam)))
    k = (-0.5 * RG_C * LOG2_E) * softplus
    ub_scr[...] = u_ref[...].astype(BF16)
    n_rb = ROWS // SCAN_RB
    h = h_scr[...]
    for rb in (reversed(range(n_rb)) if reverse else range(n_rb)):
        rows = slice(rb * SCAN_RB, (rb + 1) * SCAN_RB)
        for j in range(N_BD):
            cols = slice(j * MXU_N, (j + 1) * MXU_N)
            pre = jnp.dot(ub_scr[rows, BD_K0[j]:BD_K0[j] + BD_WIN], wab_ref[j],
                          preferred_element_type=F32)
            t_r = jnp.tanh(pre[:, :MXU_N] + bab_ref[0:1, cols])
            t_i = jnp.tanh(pre[:, MXU_N:] + bab_ref[1:2, cols])
            a = jnp.exp2(k[:, cols] * t_r + k[:, cols])
            w = 1.0 - a * a
            sq = jnp.where(w > 0.0, w * lax.rsqrt(w), 0.0)
            a_scr[rows, cols] = a
            b_scr[rows, cols] = (0.5 * sq) * ((t_i + 1.0) * u_ref[rows, cols])
        steps = range(rb * SCAN_RB // BATCH, (rb + 1) * SCAN_RB // BATCH)
        for t in (reversed(steps) if reverse else steps):
            r8 = slice(t * BATCH, (t + 1) * BATCH)
            h = a_scr[r8, :] * h + b_scr[r8, :]
            b_scr[r8, :] = h
    h_scr[...] = h
    return h


def _proj_kernel(*refs, nchunks, add_pos, want_gate):
    it = iter(refs)
    xm_ref, xp_ref, xn_ref = next(it), next(it), next(it)
    if add_pos:
        rowtab_ref, coltab_ref = next(it), next(it)
    g_ref, sh_ref, sc_ref = next(it), next(it), next(it)
    wr_ref, cw_ref, cb_ref = next(it), next(it), next(it)
    h0_ref, lam_ref, wab_ref, bab_ref = next(it), next(it), next(it), next(it)
    if want_gate:
        wg_ref = next(it)
    if add_pos:
        x0_ref = next(it)
    u_ref = next(it)
    if want_gate:
        gate_ref = next(it)
        yf_ref = next(it)
    hT_ref = next(it)
    zr_scr, ub_scr, a_scr, b_scr, h_scr = next(it), next(it), next(it), next(it), next(it)

    i = pl.program_id(0)

    @pl.when(i == 0)
    def _():
        h_scr[...] = h0_ref[...]

    xm = jnp.swapaxes(xm_ref[...], 0, 1)
    xp = jnp.swapaxes(xp_ref[...], 0, 1)[XHALO_T - HALO_T:]
    xn = jnp.swapaxes(xn_ref[...], 0, 1)[:HALO_T]
    x3 = jnp.concatenate([xp, xm, xn], axis=0)
    if add_pos:
        half = D_MODEL // 2
        r_prev = rowtab_ref[pl.ds(jnp.maximum(i - 1, 0), 1)]
        r_here = rowtab_ref[pl.ds(i, 1)]
        r_next = rowtab_ref[pl.ds(jnp.minimum(i + 1, nchunks - 1), 1)]
        prow = jnp.concatenate([jnp.broadcast_to(r_prev, (HALO_T, 1, half)),
                                jnp.broadcast_to(r_here, (TT, 1, half)),
                                jnp.broadcast_to(r_next, (HALO_T, 1, half))], axis=0)
        pcol = jnp.concatenate([coltab_ref[GRID_W - HALO_T:], coltab_ref[...], coltab_ref[:HALO_T]], axis=0)
        x3 = x3 + jnp.concatenate([prow, pcol], axis=-1)
        x0_ref[...] = x3[HALO_T:HALO_T + TT].reshape(ROWS, D_MODEL)
    h = _norm_mod(x3, g_ref[...], sh_ref[...], sc_ref[...])
    keep_prev = jnp.where(i == 0, 0.0, 1.0)
    keep_next = jnp.where(i == nchunks - 1, 0.0, 1.0)
    h = jnp.concatenate([h[:HALO_T] * keep_prev, h[HALO_T:HALO_T + TT], h[HALO_T + TT:] * keep_next], axis=0)
    h = h.reshape(ROWS + 2 * HALO, D_MODEL).astype(BF16)

    zr_scr[...] = jnp.dot(h, wr_ref[...], preferred_element_type=F32)
    if want_gate:
        zg = jnp.dot(h[HALO:HALO + ROWS], wg_ref[...], preferred_element_type=F32)
        gate_ref[...] = jax.nn.gelu(zg).astype(BF16)

    _fir(zr_scr, u_ref, cw_ref, cb_ref, n_taps=REC_CONV_W, src_off=HALO - BATCH, n_out=TT, width=D_RNN, block=16,
         unrolled=True)

    hT_ref[...] = _rglru(u_ref, ub_scr, lam_ref, wab_ref, bab_ref, a_scr, b_scr, h_scr, reverse=False)
    if want_gate:
        yf_ref[...] = b_scr[...]


def _rec_project(x, pos_tabs, mods, ctx, norm_g, w_in, conv_w, conv_b, h0, lam, wab, bab, want_gate):
    t_len = x.shape[1]
    n = t_len * BATCH
    nchunks = t_len // TT
    add_pos = pos_tabs is not None
    hb = TT // XHALO_T
    nhb = t_len // XHALO_T

    in_specs = [
        pl.BlockSpec((BATCH, TT, D_MODEL), lambda i: (0, i, 0)),
        pl.BlockSpec((BATCH, XHALO_T, D_MODEL), lambda i: (0, jnp.maximum(i * hb - 1, 0), 0)),
        pl.BlockSpec((BATCH, XHALO_T, D_MODEL), lambda i: (0, jnp.minimum((i + 1) * hb, nhb - 1), 0)),
    ]
    args = [x, x, x]
    if add_pos:
        assert TT == GRID_W and pos_tabs[0].shape[0] == nchunks
        in_specs += [_const_spec(pos_tabs[0].shape), _const_spec(pos_tabs[1].shape)]
        args += list(pos_tabs)
    in_specs += [
        _layer_spec((1, D_MODEL), 0, 0, 0),
        _mod_spec(0, ctx, 0),
        _mod_spec(0, ctx, 1),
        _layer_spec((D_MODEL, D_RNN), 0, 0, 1),
        _layer_spec((REC_CONV_W, D_RNN), 0, 0, 0),
        _layer_spec((1, D_RNN), 0, 0, 0),
    ]
    args += [norm_g, mods, mods, w_in, conv_w, conv_b]
    in_specs += [
        _const_spec((BATCH, D_RNN)),
        _layer_spec((1, D_RNN), 0, 0, 0),
        _layer_spec((N_BD, BD_WIN, 2 * MXU_N), 0, 0, 0, 0),
        _layer_spec((2, D_RNN), 0, 0, 0),
    ]
    args += [h0, lam, wab, bab]
    if want_gate:
        in_specs.append(_layer_spec((D_MODEL, D_RNN), 0, 0, 0))
        args.append(w_in)

    out_specs, out_shape = [], []
    if add_pos:
        out_specs.append(pl.BlockSpec((ROWS, D_MODEL), lambda i: (i, 0)))
        out_shape.append(jax.ShapeDtypeStruct((n, D_MODEL), F32))
    out_specs.append(pl.BlockSpec((ROWS, D_RNN), lambda i: (i, 0)))
    out_shape.append(jax.ShapeDtypeStruct((n, D_RNN), F32))
    if want_gate:
        out_specs.append(pl.BlockSpec((ROWS, D_RNN), lambda i: (i, 0)))
        out_shape.append(jax.ShapeDtypeStruct((n, D_RNN), BF16))
        out_specs.append(pl.BlockSpec((ROWS, D_RNN), lambda i: (i, 0)))
        out_shape.append(jax.ShapeDtypeStruct((n, D_RNN), F32))
    out_specs.append(pl.BlockSpec((BATCH, D_RNN), lambda i: (0, 0)))
    out_shape.append(jax.ShapeDtypeStruct((BATCH, D_RNN), F32))

    return pl.pallas_call(
        functools.partial(_proj_kernel, nchunks=nchunks, add_pos=add_pos, want_gate=want_gate),
        grid=(nchunks,),
        in_specs=in_specs,
        out_specs=out_specs,
        out_shape=out_shape,
        scratch_shapes=[
            pltpu.VMEM((ROWS + 2 * HALO, D_RNN), F32),
            pltpu.VMEM((ROWS, D_RNN), BF16),
            pltpu.VMEM((ROWS, D_RNN), F32),
            pltpu.VMEM((ROWS, D_RNN), F32),
            pltpu.VMEM((BATCH, D_RNN), F32),
        ],
        compiler_params=_cparams(),
        name="rec_fwd_lat" if add_pos else "rec_fwd_ctx",
    )(*args)


def _scan_kernel(*refs, mode):
    it = iter(refs)
    u_ref, h0_ref, lam_ref, wab_ref, bab_ref = next(it), next(it), next(it), next(it), next(it)
    if mode == "out":
        yf_ref, gate_ref, x0_ref, g1_ref, wo_ref = next(it), next(it), next(it), next(it), next(it)
        x1_ref = next(it)
    hT_ref = next(it)
    ub_scr, a_scr, b_scr, h_scr = next(it), next(it), next(it), next(it)

    i = pl.program_id(0)

    @pl.when(i == 0)
    def _():
        h_scr[...] = h0_ref[...]

    hT_ref[...] = _rglru(u_ref, ub_scr, lam_ref, wab_ref, bab_ref, a_scr, b_scr, h_scr, reverse=True)

    if mode == "out":
        y = b_scr[...] + yf_ref[...]
        gy = (gate_ref[...].astype(F32) * y).astype(BF16)
        o = jnp.dot(gy, wo_ref[...], preferred_element_type=F32)
        x1 = x0_ref[...].reshape(TT, BATCH, D_MODEL) + g1_ref[...] * o.reshape(TT, BATCH, D_MODEL)
        x1_ref[...] = x1.reshape(ROWS, D_MODEL)


def _rec_scan_rev(u, h0, lam, wab, bab, *, mode, extra=None, mods=None):
    n = u.shape[0]
    nchunks = n // ROWS
    order = lambda i: (nchunks - 1 - i, 0)
    in_specs = [
        pl.BlockSpec((ROWS, D_RNN), order),
        _const_spec((BATCH, D_RNN)),
        _layer_spec((1, D_RNN), 1, 0, 0),
        _layer_spec((N_BD, BD_WIN, 2 * MXU_N), 1, 0, 0, 0),
        _layer_spec((2, D_RNN), 1, 0, 0),
    ]
    args = [u, h0, lam, wab, bab]
    out_specs, out_shape = [], []
    if mode == "out":
        yf, gate, x0, w_out = extra
        in_specs += [
            pl.BlockSpec((ROWS, D_RNN), order),
            pl.BlockSpec((ROWS, D_RNN), order),
            pl.BlockSpec((ROWS, D_MODEL), order),
            _mod_spec(0, False, 2),
            _layer_spec((D_RNN, D_MODEL), 0, 0, 0),
        ]
        args += [yf, gate, x0, mods, w_out]
        out_specs.append(pl.BlockSpec((ROWS, D_MODEL), order))
        out_shape.append(jax.ShapeDtypeStruct((n, D_MODEL), F32))
    out_specs.append(pl.BlockSpec((BATCH, D_RNN), lambda i: (0, 0)))
    out_shape.append(jax.ShapeDtypeStruct((BATCH, D_RNN), F32))

    return pl.pallas_call(
        functools.partial(_scan_kernel, mode=mode),
        grid=(nchunks,),
        in_specs=in_specs,
        out_specs=out_specs,
        out_shape=out_shape,
        scratch_shapes=[
            pltpu.VMEM((ROWS, D_RNN), BF16),
            pltpu.VMEM((ROWS, D_RNN), F32),
            pltpu.VMEM((ROWS, D_RNN), F32),
            pltpu.VMEM((BATCH, D_RNN), F32),
        ],
        compiler_params=_cparams(),
        name="rec_rev_" + mode,
    )(*args)


def _mlp_kernel(*refs, final):
    it = iter(refs)
    x_ref, g_ref, sh_ref, sc_ref, gt_ref, w1_ref, w2_ref = (next(it) for _ in range(7))
    if final:
        fg_ref = next(it)
    o_ref = next(it)

    x3 = x_ref[...].reshape(TT, BATCH, D_MODEL)
    h = _norm_mod(x3, g_ref[...], sh_ref[...], sc_ref[...]).reshape(ROWS, D_MODEL).astype(BF16)
    m = jnp.dot(h, w1_ref[...], preferred_element_type=F32)
    m = jnp.square(jnp.maximum(m, 0.0)).astype(BF16)
    o = jnp.dot(m, w2_ref[...], preferred_element_type=F32)
    y = x3 + gt_ref[...] * o.reshape(TT, BATCH, D_MODEL)
    if final:
        o_ref[...] = jnp.swapaxes(_rms(y, fg_ref[...]), 0, 1)
    else:
        o_ref[...] = y.reshape(ROWS, D_MODEL)


def _mlp(x, mods, layer, norm_g, w1, w2, final_g=None):
    n = x.shape[0]
    final = final_g is not None
    in_specs = [
        pl.BlockSpec((ROWS, D_MODEL), lambda i: (i, 0)),
        _layer_spec((1, D_MODEL), 2 * layer + 1, 0, 0),
        _mod_spec(layer, False, 3),
        _mod_spec(layer, False, 4),
        _mod_spec(layer, False, 5),
        _layer_spec((D_MODEL, D_FF), layer, 0, 0),
        _layer_spec((D_FF, D_MODEL), layer, 0, 0),
    ]
    args = [x, norm_g, mods, mods, mods, w1, w2]
    if final:
        in_specs.append(_const_spec((1, D_MODEL)))
        args.append(final_g)
        out_spec = pl.BlockSpec((BATCH, TT, D_MODEL), lambda i: (0, i, 0))
        out_shape = jax.ShapeDtypeStruct((BATCH, n // BATCH, D_MODEL), F32)
    else:
        out_spec = pl.BlockSpec((ROWS, D_MODEL), lambda i: (i, 0))
        out_shape = jax.ShapeDtypeStruct((n, D_MODEL), F32)
    return pl.pallas_call(
        functools.partial(_mlp_kernel, final=final),
        grid=(n // ROWS,),
        in_specs=in_specs,
        out_specs=out_spec,
        out_shape=out_shape,
        compiler_params=_cparams(),
        name="mlp_final" if final else "mlp",
    )(*args)


def _conf_in_kernel(x_ref, g_ref, sh_ref, sc_ref, w_ref, b_ref, z_ref):
    x3 = x_ref[...].reshape(TT, BATCH, D_MODEL)
    h = _norm_mod(x3, g_ref[...], sh_ref[...], sc_ref[...]).reshape(ROWS, D_MODEL).astype(BF16)
    z = jnp.dot(h, w_ref[...], preferred_element_type=F32) + b_ref[...]
    z_ref[...] = (z[:, :D_MODEL] * _sigmoid(z[:, D_MODEL:])).astype(BF16)


def _conf_in(x, mods, layer, norm_g, w_pw1, b_pw1):
    n = x.shape[0]
    return pl.pallas_call(
        _conf_in_kernel,
        grid=(n // ROWS,),
        in_specs=[
            pl.BlockSpec((ROWS, D_MODEL), lambda i: (i, 0)),
            _layer_spec((1, D_MODEL), 2 * layer, 0, 0),
            _mod_spec(layer, False, 0),
            _mod_spec(layer, False, 1),
            _layer_spec((D_MODEL, 2 * D_MODEL), 0, 0, 0),
            _const_spec((1, 2 * D_MODEL)),
        ],
        out_specs=pl.BlockSpec((ROWS, D_MODEL), lambda i: (i, 0)),
        out_shape=jax.ShapeDtypeStruct((n, D_MODEL), BF16),
        compiler_params=_cparams(),
        name="conf_in",
    )(x, norm_g, mods, mods, w_pw1, b_pw1)


def _conf_out_kernel(zm_ref, zp_ref, zn_ref, x_ref, cw_ref, cb_ref, lg_ref, lb_ref, w_ref, b_ref, g1_ref,
                     o_ref, zext_scr, conv_scr, *, nchunks):
    i = pl.program_id(0)
    zext_scr[0:CONF_HALO, :] = zp_ref[...].astype(F32)
    zext_scr[CONF_HALO:CONF_HALO + ROWS, :] = zm_ref[...].astype(F32)
    zext_scr[CONF_HALO + ROWS:, :] = zn_ref[...].astype(F32)

    @pl.when(i == 0)
    def _():
        zext_scr[0:CONF_HALO, :] = jnp.zeros((CONF_HALO, D_MODEL), F32)

    @pl.when(i == nchunks - 1)
    def _():
        zext_scr[CONF_HALO + ROWS:, :] = jnp.zeros((CONF_HALO, D_MODEL), F32)

    _fir(zext_scr, conv_scr, cw_ref, cb_ref, n_taps=CONF_KW, src_off=CONF_HALO - CONF_HALF * BATCH,
         n_out=TT, width=D_MODEL, block=16, unrolled=False)

    z = conv_scr[...]
    mu = jnp.mean(z, axis=-1, keepdims=True)
    zc = z - mu
    var = jnp.mean(zc * zc, axis=-1, keepdims=True)
    zl = zc * lax.rsqrt(var + EPS) * lg_ref[...] + lb_ref[...]
    zs = (zl * _sigmoid(zl)).astype(BF16)
    y = jnp.dot(zs, w_ref[...], preferred_element_type=F32) + b_ref[...]
    x1 = x_ref[...].reshape(TT, BATCH, D_MODEL) + g1_ref[...] * y.reshape(TT, BATCH, D_MODEL)
    o_ref[...] = x1.reshape(ROWS, D_MODEL)


def _conf_out(z, x, mods, layer, conv_w, conv_b, ln_g, ln_b, w_pw2, b_pw2):
    n = x.shape[0]
    nchunks = n // ROWS
    hb = ROWS // CONF_HALO
    return pl.pallas_call(
        functools.partial(_conf_out_kernel, nchunks=nchunks),
        grid=(nchunks,),
        in_specs=[
            pl.BlockSpec((ROWS, D_MODEL), lambda i: (i, 0)),
            pl.BlockSpec((CONF_HALO, D_MODEL), lambda i: (jnp.maximum(i * hb - 1, 0), 0)),
            pl.BlockSpec((CONF_HALO, D_MODEL), lambda i: (jnp.minimum((i + 1) * hb, n // CONF_HALO - 1), 0)),
            pl.BlockSpec((ROWS, D_MODEL), lambda i: (i, 0)),
            _layer_spec((CONF_KW, D_MODEL), 0, 0, 0),
            _const_spec((1, D_MODEL)),
            _const_spec((1, D_MODEL)),
            _const_spec((1, D_MODEL)),
            _layer_spec((D_MODEL, D_MODEL), 0, 0, 0),
            _const_spec((1, D_MODEL)),
            _mod_spec(layer, False, 2),
        ],
        out_specs=pl.BlockSpec((ROWS, D_MODEL), lambda i: (i, 0)),
        out_shape=jax.ShapeDtypeStruct((n, D_MODEL), F32),
        scratch_shapes=[
            pltpu.VMEM((ROWS + 2 * CONF_HALO, D_MODEL), F32),
            pltpu.VMEM((ROWS, D_MODEL), F32),
        ],
        compiler_params=_cparams(),
        name="conf_out",
    )(z, z, z, x, conv_w, conv_b, ln_g, ln_b, w_pw2, b_pw2, mods)


def _pos_tables(rows, d):
    q = d // 4
    omega = 1.0 / (POS_BASE ** (jnp.arange(q, dtype=F32) / q))
    er = jnp.arange(rows, dtype=jnp.int32).astype(F32)[:, None] * omega[None, :]
    ec = jnp.arange(GRID_W, dtype=jnp.int32).astype(F32)[:, None] * omega[None, :]
    rowtab = jnp.concatenate([jnp.sin(er), jnp.cos(er)], axis=-1)
    coltab = jnp.concatenate([jnp.sin(ec), jnp.cos(ec)], axis=-1)
    return rowtab.reshape(rows, 1, d // 2), coltab.reshape(GRID_W, 1, d // 2)


def _block_diag_windows(w_a, w_x):
    w = jnp.stack([w_a, w_x], axis=1).reshape(2, 2, D_RNN, RNN_BLOCK)
    tiled_eye = np.tile(np.eye(RNN_BLOCK, dtype=np.float32), (1, N_RNN_BLOCKS))
    row_blk = np.arange(D_RNN)[:, None] // RNN_BLOCK
    col_blk = np.arange(D_RNN)[None, :] // RNN_BLOCK
    w_win, sel, msk = [], [], []
    for j, k0 in enumerate(BD_K0):
        cols = slice(j * MXU_N, (j + 1) * MXU_N)
        w_win.append(w[:, :, k0:k0 + BD_WIN])
        sel.append(tiled_eye[:, cols])
        msk.append(0.5 * (row_blk[k0:k0 + BD_WIN] == col_blk[:, cols]))
    w_win = jnp.stack(w_win, axis=2)
    sel = jnp.asarray(np.stack(sel))
    msk = jnp.asarray(np.stack(msk), dtype=F32)
    full = jnp.einsum("dgjrk,jkc->djrgc", w_win, sel) * msk[None, :, :, None, :]
    return full.reshape(2, N_BD, BD_WIN, 2 * MXU_N).astype(BF16)


def kernel(x, c, ctx, c_ctx, w_ada, b_ada, norm_g, rec_w_in, rec_conv_w, rec_conv_b, rec_lambda, rec_w_a, rec_b_a,
           rec_w_x, rec_b_x, rec_w_out, conf_w_pw1, conf_b_pw1, conf_conv_w, conf_conv_b, conf_ln_g, conf_ln_b,
           conf_w_pw2, conf_b_pw2, mlp_w_in, mlp_w_out, final_g):
    assert x.shape == (BATCH, SEQ, D_MODEL) and ctx.shape == (BATCH, CTX_LEN, D_MODEL)
    assert w_ada.shape[0] == 2, "one recurrent layer followed by one conformer layer"

    cc = jnp.concatenate([c, jnp.broadcast_to(c_ctx[None, :], (BATCH, D_MODEL))], axis=0)
    mods = _ada_table(cc, w_ada, b_ada)
    norm_g = norm_g.reshape(4, 1, D_MODEL)
    mlp_w1, mlp_w2 = mlp_w_in.astype(BF16), mlp_w_out.astype(BF16)

    w_in = rec_w_in.astype(BF16)
    conv_b = rec_conv_b.reshape(1, 1, D_RNN)
    wab = _block_diag_windows(rec_w_a[0], rec_w_x[0])
    bab = 0.5 * jnp.stack([rec_b_a[0].reshape(2, D_RNN), rec_b_x[0].reshape(2, D_RNN)], axis=1)
    lam = rec_lambda[0].reshape(2, 1, D_RNN)
    zeros = jnp.zeros((BATCH, D_RNN), F32)
    rec = (lam, wab, bab)

    u_ctx, h0_f = _rec_project(ctx, None, mods, True, norm_g, w_in, rec_conv_w, conv_b, zeros, *rec, False)
    (h0_r,) = _rec_scan_rev(u_ctx, zeros, *rec, mode="state")

    pos_tabs = _pos_tables(SEQ // GRID_W, D_MODEL)
    x0, u_lat, gate, y_f, _ = _rec_project(x, pos_tabs, mods, False, norm_g, w_in, rec_conv_w, conv_b, h0_f, *rec,
                                           True)
    x1, _ = _rec_scan_rev(u_lat, h0_r, *rec, mode="out", extra=(y_f, gate, x0, rec_w_out.astype(BF16)), mods=mods)
    x2 = _mlp(x1, mods, 0, norm_g, mlp_w1, mlp_w2)

    z = _conf_in(x2, mods, 1, norm_g, conf_w_pw1.astype(BF16), conf_b_pw1.reshape(1, 2 * D_MODEL))
    x3 = _conf_out(z, x2, mods, 1, conf_conv_w, conf_conv_b.reshape(1, D_MODEL),
                   conf_ln_g.reshape(1, D_MODEL), conf_ln_b.reshape(1, D_MODEL),
                   conf_w_pw2.astype(BF16), conf_b_pw2.reshape(1, D_MODEL))
    return _mlp(x3, mods, 1, norm_g, mlp_w1, mlp_w2, final_g=final_g.reshape(1, D_MODEL))
```

```python
import functools

import jax
import jax.numpy as jnp
import numpy as np
from jax import lax
from jax.experimental import pallas as pl
from jax.experimental.pallas import tpu as pltpu

F32 = jnp.float32
BF16 = jnp.bfloat16

D_MODEL = 1024
BATCH = 8
SEQ = 2048
CTX_LEN = 256
GRID_W = 64
D_RNN = 1280
N_RNN_BLOCKS = 16
RNN_BLOCK = D_RNN // N_RNN_BLOCKS
REC_CONV_W = 4
RG_C = 8.0
CONF_KW = 31
CONF_HALF = CONF_KW // 2
D_FF = 4 * D_MODEL
N_MOD = 6
EPS = 1e-6
POS_BASE = 10000.0

VMEM_LIMIT_BYTES = 56 * 1024 * 1024
LANES = 128
MXU_N = 256
BD_WIN = 512
BD_K0 = (0, 128, 384, 640, 768)
N_BD = D_RNN // MXU_N

TT = 64
ROWS = TT * BATCH
HALO = 16
HALO_T = HALO // BATCH
XHALO_T = 8
CONF_HALO = 128


def _cparams():
    return pltpu.CompilerParams(dimension_semantics=("arbitrary",), vmem_limit_bytes=VMEM_LIMIT_BYTES)


def _const_spec(shape):
    return pl.BlockSpec(shape, lambda *_: (0,) * len(shape), pipeline_mode=pl.Buffered(1))


def _layer_spec(shape, *idx):
    return pl.BlockSpec((None,) + tuple(shape), lambda *_: tuple(idx), pipeline_mode=pl.Buffered(1))


def _mod_spec(layer, ctx, k):
    return _layer_spec((BATCH, D_MODEL), layer, int(ctx), k)


def _sigmoid(x):
    return 0.5 * jnp.tanh(0.5 * x) + 0.5


def _rms(x3, g):
    ms = jnp.mean(x3 * x3, axis=-1, keepdims=True)
    return x3 * lax.rsqrt(ms + EPS) * g


def _norm_mod(x3, g, sh, sc):
    return _rms(x3, g) * (1.0 + sc) + sh


def _ada_kernel(c_ref, w_ref, b_ref, o_ref):
    c = c_ref[...]
    s = (c * jax.nn.sigmoid(c)).astype(BF16)
    o_ref[...] = jnp.dot(s, w_ref[...].astype(BF16), preferred_element_type=F32) + b_ref[...]


def _ada_table(cc, w_ada, b_ada):
    depth = w_ada.shape[0]
    tn = 1536
    return pl.pallas_call(
        _ada_kernel,
        grid=(depth, N_MOD * D_MODEL // tn),
        in_specs=[
            pl.BlockSpec((2 * BATCH, D_MODEL), lambda l, j: (0, 0)),
            pl.BlockSpec((None, D_MODEL, tn), lambda l, j: (l, 0, j)),
            pl.BlockSpec((None, 1, tn), lambda l, j: (l, 0, j)),
        ],
        out_specs=pl.BlockSpec((None, 2 * BATCH, tn), lambda l, j: (l, 0, j)),
        out_shape=jax.ShapeDtypeStruct((depth, 2 * BATCH, N_MOD * D_MODEL), F32),
        compiler_params=pltpu.CompilerParams(
            dimension_semantics=("arbitrary", "arbitrary"), vmem_limit_bytes=VMEM_LIMIT_BYTES),
        name="ada_table",
    )(cc, w_ada, b_ada.reshape(depth, 1, N_MOD * D_MODEL))


def _fir(src_ref, dst_ref, w_ref, b_ref, *, n_taps, src_off, n_out, width, block, unrolled):
    for c in range(width // LANES):
        lanes = slice(c * LANES, (c + 1) * LANES)
        wk = [jnp.broadcast_to(w_ref[k:k + 1, lanes], (BATCH, LANES)) for k in range(n_taps)]
        bias = jnp.broadcast_to(b_ref[:, lanes], (BATCH, LANES))

        def one_block(base):
            acc = [bias] * block
            for q in range(block + n_taps - 1):
                z = src_ref[pl.ds(base + src_off + q * BATCH, BATCH), lanes]
                for j in range(max(0, q - n_taps + 1), min(block, q + 1)):
                    acc[j] = acc[j] + wk[q - j] * z
            for j in range(block):
                dst_ref[pl.ds(base + j * BATCH, BATCH), lanes] = acc[j]

        if unrolled:
            for blk in range(n_out // block):
                one_block(blk * block * BATCH)
        else:
            def body(blk, carry):
                one_block(pl.multiple_of(blk * (block * BATCH), block * BATCH))
                return carry

            lax.fori_loop(0, n_out // block, body, 0, unroll=2)


SCAN_RB = 128
LOG2_E = 1.4426950408889634


def _rglru(u_ref, ub_ref, lam_ref, wab_ref, bab_ref, a_scr, b_scr, h_scr, reverse):
    lam = lam_ref[...]
    softplus = jnp.maximum(-lam, 0.0) + jnp.log1p(jnp.exp(-jnp.abs(lam)))
    k = (-0.5 * RG_C * LOG2_E) * softplus
    n_rb = ROWS // SCAN_RB
    h = h_scr[...]
    for rb in (reversed(range(n_rb)) if reverse else range(n_rb)):
        rows = slice(rb * SCAN_RB, (rb + 1) * SCAN_RB)
        for j in range(N_BD):
            cols = slice(j * MXU_N, (j + 1) * MXU_N)
            pre = jnp.dot(ub_ref[rows, BD_K0[j]:BD_K0[j] + BD_WIN], wab_ref[j],
                          preferred_element_type=F32)
            u = u_ref[rows, cols] if u_ref is not None else ub_ref[rows, cols].astype(F32)
            t_r = jnp.tanh(pre[:, :MXU_N] + bab_ref[0:1, cols])
            t_i = jnp.tanh(pre[:, MXU_N:] + bab_ref[1:2, cols])
            a = jnp.exp2(k[:, cols] * t_r + k[:, cols])
            w = 1.0 - a * a
            sq = jnp.where(w > 0.0, w * lax.rsqrt(w), 0.0)
            a_scr[rows, cols] = a
            b_scr[rows, cols] = (0.5 * sq) * ((t_i + 1.0) * u)
        steps = range(rb * SCAN_RB // BATCH, (rb + 1) * SCAN_RB // BATCH)
        for t in (reversed(steps) if reverse else steps):
            r8 = slice(t * BATCH, (t + 1) * BATCH)
            h = a_scr[r8, :] * h + b_scr[r8, :]
            b_scr[r8, :] = h
    h_scr[...] = h
    return h


def _proj_kernel(*refs, nchunks, add_pos, want_gate):
    it = iter(refs)
    xm_ref, xp_ref, xn_ref = next(it), next(it), next(it)
    if add_pos:
        rowtab_ref, coltab_ref = next(it), next(it)
    g_ref, sh_ref, sc_ref = next(it), next(it), next(it)
    wr_ref, cw_ref, cb_ref = next(it), next(it), next(it)
    h0_ref, lam_ref, wab_ref, bab_ref = next(it), next(it), next(it), next(it)
    if want_gate:
        wg_ref = next(it)
    if add_pos:
        x0_ref = next(it)
    u_ref = next(it)
    if want_gate:
        gate_ref = next(it)
        yf_ref = next(it)
    hT_ref = next(it)
    zr_scr, u_scr, a_scr, b_scr, h_scr = next(it), next(it), next(it), next(it), next(it)

    i = pl.program_id(0)

    @pl.when(i == 0)
    def _():
        h_scr[...] = h0_ref[...]

    xm = jnp.swapaxes(xm_ref[...], 0, 1)
    xp = jnp.swapaxes(xp_ref[...], 0, 1)[XHALO_T - HALO_T:]
    xn = jnp.swapaxes(xn_ref[...], 0, 1)[:HALO_T]
    x3 = jnp.concatenate([xp, xm, xn], axis=0)
    if add_pos:
        half = D_MODEL // 2
        r_prev = rowtab_ref[pl.ds(jnp.maximum(i - 1, 0), 1)]
        r_here = rowtab_ref[pl.ds(i, 1)]
        r_next = rowtab_ref[pl.ds(jnp.minimum(i + 1, nchunks - 1), 1)]
        prow = jnp.concatenate([jnp.broadcast_to(r_prev, (HALO_T, 1, half)),
                                jnp.broadcast_to(r_here, (TT, 1, half)),
                                jnp.broadcast_to(r_next, (HALO_T, 1, half))], axis=0)
        pcol = jnp.concatenate([coltab_ref[GRID_W - HALO_T:], coltab_ref[...], coltab_ref[:HALO_T]], axis=0)
        x3 = x3 + jnp.concatenate([prow, pcol], axis=-1)
        x0_ref[...] = x3[HALO_T:HALO_T + TT].reshape(ROWS, D_MODEL)
    h = _norm_mod(x3, g_ref[...], sh_ref[...], sc_ref[...])
    keep_prev = jnp.where(i == 0, 0.0, 1.0)
    keep_next = jnp.where(i == nchunks - 1, 0.0, 1.0)
    h = jnp.concatenate([h[:HALO_T] * keep_prev, h[HALO_T:HALO_T + TT], h[HALO_T + TT:] * keep_next], axis=0)
    h = h.reshape(ROWS + 2 * HALO, D_MODEL).astype(BF16)

    zr_scr[...] = jnp.dot(h, wr_ref[...], preferred_element_type=F32)
    if want_gate:
        zg = jnp.dot(h[HALO:HALO + ROWS], wg_ref[...], preferred_element_type=F32)
        gate_ref[...] = jax.nn.gelu(zg).astype(BF16)

    _fir(zr_scr, u_scr, cw_ref, cb_ref, n_taps=REC_CONV_W, src_off=HALO - BATCH, n_out=TT, width=D_RNN, block=16,
         unrolled=True)
    u_ref[...] = u_scr[...].astype(BF16)

    hT_ref[...] = _rglru(u_scr, u_ref, lam_ref, wab_ref, bab_ref, a_scr, b_scr, h_scr, reverse=False)
    if want_gate:
        yf_ref[...] = b_scr[...].astype(BF16)


def _rec_project(x, pos_tabs, mods, ctx, norm_g, w_in, conv_w, conv_b, h0, lam, wab, bab, want_gate):
    t_len = x.shape[1]
    n = t_len * BATCH
    nchunks = t_len // TT
    add_pos = pos_tabs is not None
    hb = TT // XHALO_T
    nhb = t_len // XHALO_T

    in_specs = [
        pl.BlockSpec((BATCH, TT, D_MODEL), lambda i: (0, i, 0)),
        pl.BlockSpec((BATCH, XHALO_T, D_MODEL), lambda i: (0, jnp.maximum(i * hb - 1, 0), 0)),
        pl.BlockSpec((BATCH, XHALO_T, D_MODEL), lambda i: (0, jnp.minimum((i + 1) * hb, nhb - 1), 0)),
    ]
    args = [x, x, x]
    if add_pos:
        assert TT == GRID_W and pos_tabs[0].shape[0] == nchunks
        in_specs += [_const_spec(pos_tabs[0].shape), _const_spec(pos_tabs[1].shape)]
        args += list(pos_tabs)
    in_specs += [
        _layer_spec((1, D_MODEL), 0, 0, 0),
        _mod_spec(0, ctx, 0),
        _mod_spec(0, ctx, 1),
        _layer_spec((D_MODEL, D_RNN), 0, 0, 1),
        _layer_spec((REC_CONV_W, D_RNN), 0, 0, 0),
        _layer_spec((1, D_RNN), 0, 0, 0),
    ]
    args += [norm_g, mods, mods, w_in, conv_w, conv_b]
    in_specs += [
        _const_spec((BATCH, D_RNN)),
        _layer_spec((1, D_RNN), 0, 0, 0),
        _layer_spec((N_BD, BD_WIN, 2 * MXU_N), 0, 0, 0, 0),
        _layer_spec((2, D_RNN), 0, 0, 0),
    ]
    args += [h0, lam, wab, bab]
    if want_gate:
        in_specs.append(_layer_spec((D_MODEL, D_RNN), 0, 0, 0))
        args.append(w_in)

    out_specs, out_shape = [], []
    if add_pos:
        out_specs.append(pl.BlockSpec((ROWS, D_MODEL), lambda i: (i, 0)))
        out_shape.append(jax.ShapeDtypeStruct((n, D_MODEL), F32))
    out_specs.append(pl.BlockSpec((ROWS, D_RNN), lambda i: (i, 0)))
    out_shape.append(jax.ShapeDtypeStruct((n, D_RNN), BF16))
    if want_gate:
        out_specs.append(pl.BlockSpec((ROWS, D_RNN), lambda i: (i, 0)))
        out_shape.append(jax.ShapeDtypeStruct((n, D_RNN), BF16))
        out_specs.append(pl.BlockSpec((ROWS, D_RNN), lambda i: (i, 0)))
        out_shape.append(jax.ShapeDtypeStruct((n, D_RNN), BF16))
    out_specs.append(pl.BlockSpec((BATCH, D_RNN), lambda i: (0, 0)))
    out_shape.append(jax.ShapeDtypeStruct((BATCH, D_RNN), F32))

    return pl.pallas_call(
        functools.partial(_proj_kernel, nchunks=nchunks, add_pos=add_pos, want_gate=want_gate),
        grid=(nchunks,),
        in_specs=in_specs,
        out_specs=out_specs,
        out_shape=out_shape,
        scratch_shapes=[
            pltpu.VMEM((ROWS + 2 * HALO, D_RNN), F32),
            pltpu.VMEM((ROWS, D_RNN), F32),
            pltpu.VMEM((ROWS, D_RNN), F32),
            pltpu.VMEM((ROWS, D_RNN), F32),
            pltpu.VMEM((BATCH, D_RNN), F32),
        ],
        compiler_params=_cparams(),
        name="rec_fwd_lat" if add_pos else "rec_fwd_ctx",
    )(*args)


def _mlp_block(x3, g, sh, sc, gt, w1_ref, w2_ref):
    h = _norm_mod(x3, g, sh, sc).reshape(ROWS, D_MODEL).astype(BF16)
    m = jnp.dot(h, w1_ref[...], preferred_element_type=F32)
    m = jnp.square(jnp.maximum(m, 0.0)).astype(BF16)
    o = jnp.dot(m, w2_ref[...], preferred_element_type=F32)
    return x3 + gt * o.reshape(TT, BATCH, D_MODEL)


def _scan_kernel(*refs, mode):
    it = iter(refs)
    u_ref, h0_ref, lam_ref, wab_ref, bab_ref = next(it), next(it), next(it), next(it), next(it)
    if mode == "out":
        yf_ref, gate_ref, x0_ref, g1_ref, wo_ref = next(it), next(it), next(it), next(it), next(it)
        ng_ref, sh2_ref, sc2_ref, g2_ref, w1_ref, w2_ref = (next(it) for _ in range(6))
        cg_ref, csh_ref, csc_ref, wp_ref, bp_ref = (next(it) for _ in range(5))
        x2_ref, z_ref = next(it), next(it)
    hT_ref = next(it)
    a_scr, b_scr, h_scr = next(it), next(it), next(it)

    i = pl.program_id(0)

    @pl.when(i == 0)
    def _():
        h_scr[...] = h0_ref[...]

    hT_ref[...] = _rglru(None, u_ref, lam_ref, wab_ref, bab_ref, a_scr, b_scr, h_scr, reverse=True)

    if mode == "out":
        y = b_scr[...] + yf_ref[...].astype(F32)
        gy = (gate_ref[...].astype(F32) * y).astype(BF16)
        o = jnp.dot(gy, wo_ref[...], preferred_element_type=F32)
        x1 = x0_ref[...].reshape(TT, BATCH, D_MODEL) + g1_ref[...] * o.reshape(TT, BATCH, D_MODEL)
        x2 = _mlp_block(x1, ng_ref[...], sh2_ref[...], sc2_ref[...], g2_ref[...], w1_ref, w2_ref)
        x2_ref[...] = x2.reshape(ROWS, D_MODEL)
        hc = _norm_mod(x2, cg_ref[...], csh_ref[...], csc_ref[...]).reshape(ROWS, D_MODEL).astype(BF16)
        zz = jnp.dot(hc, wp_ref[...], preferred_element_type=F32) + bp_ref[...]
        z_ref[...] = (zz[:, :D_MODEL] * _sigmoid(zz[:, D_MODEL:])).astype(BF16)


def _rec_scan_rev(u, h0, lam, wab, bab, *, mode, extra=None, mods=None):
    n = u.shape[0]
    nchunks = n // ROWS
    order = lambda i: (nchunks - 1 - i, 0)
    in_specs = [
        pl.BlockSpec((ROWS, D_RNN), order),
        _const_spec((BATCH, D_RNN)),
        _layer_spec((1, D_RNN), 1, 0, 0),
        _layer_spec((N_BD, BD_WIN, 2 * MXU_N), 1, 0, 0, 0),
        _layer_spec((2, D_RNN), 1, 0, 0),
    ]
    args = [u, h0, lam, wab, bab]
    out_specs, out_shape = [], []
    if mode == "out":
        yf, gate, x0, w_out, norm_g, w1, w2, w_pw1, b_pw1 = extra
        in_specs += [
            pl.BlockSpec((ROWS, D_RNN), order),
            pl.BlockSpec((ROWS, D_RNN), order),
            pl.BlockSpec((ROWS, D_MODEL), order),
            _mod_spec(0, False, 2),
            _layer_spec((D_RNN, D_MODEL), 0, 0, 0),
            _layer_spec((1, D_MODEL), 1, 0, 0),
            _mod_spec(0, False, 3),
            _mod_spec(0, False, 4),
            _mod_spec(0, False, 5),
            _layer_spec((D_MODEL, D_FF), 0, 0, 0),
            _layer_spec((D_FF, D_MODEL), 0, 0, 0),
            _layer_spec((1, D_MODEL), 2, 0, 0),
            _mod_spec(1, False, 0),
            _mod_spec(1, False, 1),
            _layer_spec((D_MODEL, 2 * D_MODEL), 0, 0, 0),
            _const_spec((1, 2 * D_MODEL)),
        ]
        args += [yf, gate, x0, mods, w_out, norm_g, mods, mods, mods, w1, w2, norm_g, mods, mods, w_pw1, b_pw1]
        out_specs.append(pl.BlockSpec((ROWS, D_MODEL), order))
        out_shape.append(jax.ShapeDtypeStruct((n, D_MODEL), F32))
        out_specs.append(pl.BlockSpec((ROWS, D_MODEL), order))
        out_shape.append(jax.ShapeDtypeStruct((n, D_MODEL), BF16))
    out_specs.append(pl.BlockSpec((BATCH, D_RNN), lambda i: (0, 0)))
    out_shape.append(jax.ShapeDtypeStruct((BATCH, D_RNN), F32))

    return pl.pallas_call(
        functools.partial(_scan_kernel, mode=mode),
        grid=(nchunks,),
        in_specs=in_specs,
        out_specs=out_specs,
        out_shape=out_shape,
        scratch_shapes=[
            pltpu.VMEM((ROWS, D_RNN), F32),
            pltpu.VMEM((ROWS, D_RNN), F32),
            pltpu.VMEM((BATCH, D_RNN), F32),
        ],
        compiler_params=_cparams(),
        name="rec_rev_" + mode,
    )(*args)


def _conf_out_kernel(zm_ref, zp_ref, zn_ref, x_ref, cw_ref, cb_ref, lg_ref, lb_ref, w_ref, b_ref, g1_ref,
                     ng_ref, sh2_ref, sc2_ref, g2_ref, w1_ref, w2_ref, fg_ref,
                     o_ref, zext_scr, conv_scr, *, nchunks):
    i = pl.program_id(0)
    zext_scr[0:CONF_HALO, :] = zp_ref[...].astype(F32)
    zext_scr[CONF_HALO:CONF_HALO + ROWS, :] = zm_ref[...].astype(F32)
    zext_scr[CONF_HALO + ROWS:, :] = zn_ref[...].astype(F32)

    @pl.when(i == 0)
    def _():
        zext_scr[0:CONF_HALO, :] = jnp.zeros((CONF_HALO, D_MODEL), F32)

    @pl.when(i == nchunks - 1)
    def _():
        zext_scr[CONF_HALO + ROWS:, :] = jnp.zeros((CONF_HALO, D_MODEL), F32)

    _fir(zext_scr, conv_scr, cw_ref, cb_ref, n_taps=CONF_KW, src_off=CONF_HALO - CONF_HALF * BATCH,
         n_out=TT, width=D_MODEL, block=16, unrolled=False)

    z = conv_scr[...]
    mu = jnp.mean(z, axis=-1, keepdims=True)
    zc = z - mu
    var = jnp.mean(zc * zc, axis=-1, keepdims=True)
    zl = zc * lax.rsqrt(var + EPS) * lg_ref[...] + lb_ref[...]
    zs = (zl * _sigmoid(zl)).astype(BF16)
    y = jnp.dot(zs, w_ref[...], preferred_element_type=F32) + b_ref[...]
    x3 = x_ref[...].reshape(TT, BATCH, D_MODEL) + g1_ref[...] * y.reshape(TT, BATCH, D_MODEL)
    x4 = _mlp_block(x3, ng_ref[...], sh2_ref[...], sc2_ref[...], g2_ref[...], w1_ref, w2_ref)
    o_ref[...] = jnp.swapaxes(_rms(x4, fg_ref[...]), 0, 1)


def _conf_out(z, x, mods, layer, norm_g, conv_w, conv_b, ln_g, ln_b, w_pw2, b_pw2, w1, w2, final_g):
    n = x.shape[0]
    nchunks = n // ROWS
    hb = ROWS // CONF_HALO
    return pl.pallas_call(
        functools.partial(_conf_out_kernel, nchunks=nchunks),
        grid=(nchunks,),
        in_specs=[
            pl.BlockSpec((ROWS, D_MODEL), lambda i: (i, 0)),
            pl.BlockSpec((CONF_HALO, D_MODEL), lambda i: (jnp.maximum(i * hb - 1, 0), 0)),
            pl.BlockSpec((CONF_HALO, D_MODEL), lambda i: (jnp.minimum((i + 1) * hb, n // CONF_HALO - 1), 0)),
            pl.BlockSpec((ROWS, D_MODEL), lambda i: (i, 0)),
            _layer_spec((CONF_KW, D_MODEL), 0, 0, 0),
            _const_spec((1, D_MODEL)),
            _const_spec((1, D_MODEL)),
            _const_spec((1, D_MODEL)),
            _layer_spec((D_MODEL, D_MODEL), 0, 0, 0),
            _const_spec((1, D_MODEL)),
            _mod_spec(layer, False, 2),
            _layer_spec((1, D_MODEL), 2 * layer + 1, 0, 0),
            _mod_spec(layer, False, 3),
            _mod_spec(layer, False, 4),
            _mod_spec(layer, False, 5),
            _layer_spec((D_MODEL, D_FF), layer, 0, 0),
            _layer_spec((D_FF, D_MODEL), layer, 0, 0),
            _const_spec((1, D_MODEL)),
        ],
        out_specs=pl.BlockSpec((BATCH, TT, D_MODEL), lambda i: (0, i, 0)),
        out_shape=jax.ShapeDtypeStruct((BATCH, n // BATCH, D_MODEL), F32),
        scratch_shapes=[
            pltpu.VMEM((ROWS + 2 * CONF_HALO, D_MODEL), F32),
            pltpu.VMEM((ROWS, D_MODEL), F32),
        ],
        compiler_params=_cparams(),
        name="conf_out",
    )(z, z, z, x, conv_w, conv_b, ln_g, ln_b, w_pw2, b_pw2, mods, norm_g, mods, mods, mods, w1, w2, final_g)


def _pos_tables(rows, d):
    q = d // 4
    omega = 1.0 / (POS_BASE ** (jnp.arange(q, dtype=F32) / q))
    er = jnp.arange(rows, dtype=jnp.int32).astype(F32)[:, None] * omega[None, :]
    ec = jnp.arange(GRID_W, dtype=jnp.int32).astype(F32)[:, None] * omega[None, :]
    rowtab = jnp.concatenate([jnp.sin(er), jnp.cos(er)], axis=-1)
    coltab = jnp.concatenate([jnp.sin(ec), jnp.cos(ec)], axis=-1)
    return rowtab.reshape(rows, 1, d // 2), coltab.reshape(GRID_W, 1, d // 2)


def _block_diag_windows(w_a, w_x):
    w = jnp.stack([w_a, w_x], axis=1).reshape(2, 2, D_RNN, RNN_BLOCK)
    tiled_eye = np.tile(np.eye(RNN_BLOCK, dtype=np.float32), (1, N_RNN_BLOCKS))
    row_blk = np.arange(D_RNN)[:, None] // RNN_BLOCK
    col_blk = np.arange(D_RNN)[None, :] // RNN_BLOCK
    w_win, sel, msk = [], [], []
    for j, k0 in enumerate(BD_K0):
        cols = slice(j * MXU_N, (j + 1) * MXU_N)
        w_win.append(w[:, :, k0:k0 + BD_WIN])
        sel.append(tiled_eye[:, cols])
        msk.append(0.5 * (row_blk[k0:k0 + BD_WIN] == col_blk[:, cols]))
    w_win = jnp.stack(w_win, axis=2)
    sel = jnp.asarray(np.stack(sel))
    msk = jnp.asarray(np.stack(msk), dtype=F32)
    full = jnp.einsum("dgjrk,jkc->djrgc", w_win, sel) * msk[None, :, :, None, :]
    return full.reshape(2, N_BD, BD_WIN, 2 * MXU_N).astype(BF16)


def kernel(x, c, ctx, c_ctx, w_ada, b_ada, norm_g, rec_w_in, rec_conv_w, rec_conv_b, rec_lambda, rec_w_a, rec_b_a,
           rec_w_x, rec_b_x, rec_w_out, conf_w_pw1, conf_b_pw1, conf_conv_w, conf_conv_b, conf_ln_g, conf_ln_b,
           conf_w_pw2, conf_b_pw2, mlp_w_in, mlp_w_out, final_g):
    assert x.shape == (BATCH, SEQ, D_MODEL) and ctx.shape == (BATCH, CTX_LEN, D_MODEL)
    assert w_ada.shape[0] == 2, "one recurrent layer followed by one conformer layer"

    cc = jnp.concatenate([c, jnp.broadcast_to(c_ctx[None, :], (BATCH, D_MODEL))], axis=0)
    mods = _ada_table(cc, w_ada, b_ada)
    norm_g = norm_g.reshape(4, 1, D_MODEL)
    mlp_w1, mlp_w2 = mlp_w_in.astype(BF16), mlp_w_out.astype(BF16)

    w_in = rec_w_in.astype(BF16)
    conv_b = rec_conv_b.reshape(1, 1, D_RNN)
    wab = _block_diag_windows(rec_w_a[0], rec_w_x[0])
    bab = 0.5 * jnp.stack([rec_b_a[0].reshape(2, D_RNN), rec_b_x[0].reshape(2, D_RNN)], axis=1)
    lam = rec_lambda[0].reshape(2, 1, D_RNN)
    zeros = jnp.zeros((BATCH, D_RNN), F32)
    rec = (lam, wab, bab)

    u_ctx, h0_f = _rec_project(ctx, None, mods, True, norm_g, w_in, rec_conv_w, conv_b, zeros, *rec, False)
    (h0_r,) = _rec_scan_rev(u_ctx, zeros, *rec, mode="state")

    pos_tabs = _pos_tables(SEQ // GRID_W, D_MODEL)
    x0, u_lat, gate, y_f, _ = _rec_project(x, pos_tabs, mods, False, norm_g, w_in, rec_conv_w, conv_b, h0_f, *rec,
                                           True)
    x2, z, _ = _rec_scan_rev(u_lat, h0_r, *rec, mode="out", mods=mods,
                             extra=(y_f, gate, x0, rec_w_out.astype(BF16), norm_g, mlp_w1, mlp_w2,
                                    conf_w_pw1.astype(BF16), conf_b_pw1.reshape(1, 2 * D_MODEL)))

    return _conf_out(z, x2, mods, 1, norm_g, conf_conv_w, conf_conv_b.reshape(1, D_MODEL),
                     conf_ln_g.reshape(1, D_MODEL), conf_ln_b.reshape(1, D_MODEL),
                     conf_w_pw2.astype(BF16), conf_b_pw2.reshape(1, D_MODEL), mlp_w1, mlp_w2,
                     final_g.reshape(1, D_MODEL))
```

```python
import functools

import jax
import jax.numpy as jnp
import numpy as np
from jax import lax
from jax.experimental import pallas as pl
from jax.experimental.pallas import tpu as pltpu

F32 = jnp.float32
BF16 = jnp.bfloat16

D_MODEL = 1024
BATCH = 8
SEQ = 2048
CTX_LEN = 256
GRID_W = 64
D_RNN = 1280
N_RNN_BLOCKS = 16
RNN_BLOCK = D_RNN // N_RNN_BLOCKS
REC_CONV_W = 4
RG_C = 8.0
CONF_KW = 31
CONF_HALF = CONF_KW // 2
D_FF = 4 * D_MODEL
N_MOD = 6
EPS = 1e-6
POS_BASE = 10000.0

VMEM_LIMIT_BYTES = 56 * 1024 * 1024
LANES = 128
MXU_N = 256
BD_WIN = 512
BD_K0 = (0, 128, 384, 640, 768)
N_BD = D_RNN // MXU_N

TT = 64
ROWS = TT * BATCH
HALO = 16
HALO_T = HALO // BATCH
XHALO_T = 8
CONF_HALO = 128


def _cparams():
    return pltpu.CompilerParams(dimension_semantics=("arbitrary",), vmem_limit_bytes=VMEM_LIMIT_BYTES)


def _const_spec(shape):
    return pl.BlockSpec(shape, lambda *_: (0,) * len(shape), pipeline_mode=pl.Buffered(1))


def _layer_spec(shape, *idx):
    return pl.BlockSpec((None,) + tuple(shape), lambda *_: tuple(idx), pipeline_mode=pl.Buffered(1))


def _mod_spec(layer, ctx, k):
    return _layer_spec((BATCH, D_MODEL), layer, int(ctx), k)


def _sigmoid(x):
    return 0.5 * jnp.tanh(0.5 * x) + 0.5


def _rms(x3, g):
    ms = jnp.mean(x3 * x3, axis=-1, keepdims=True)
    return x3 * lax.rsqrt(ms + EPS) * g


def _norm_mod(x3, g, sh, sc):
    return _rms(x3, g) * (1.0 + sc) + sh


def _ada_kernel(c_ref, w_ref, b_ref, o_ref):
    c = c_ref[...]
    s = (c * jax.nn.sigmoid(c)).astype(BF16)
    o_ref[...] = jnp.dot(s, w_ref[...].astype(BF16), preferred_element_type=F32) + b_ref[...]


def _ada_table(cc, w_ada, b_ada):
    depth = w_ada.shape[0]
    tn = 1536
    return pl.pallas_call(
        _ada_kernel,
        grid=(depth, N_MOD * D_MODEL // tn),
        in_specs=[
            pl.BlockSpec((2 * BATCH, D_MODEL), lambda l, j: (0, 0)),
            pl.BlockSpec((None, D_MODEL, tn), lambda l, j: (l, 0, j)),
            pl.BlockSpec((None, 1, tn), lambda l, j: (l, 0, j)),
        ],
        out_specs=pl.BlockSpec((None, 2 * BATCH, tn), lambda l, j: (l, 0, j)),
        out_shape=jax.ShapeDtypeStruct((depth, 2 * BATCH, N_MOD * D_MODEL), F32),
        compiler_params=pltpu.CompilerParams(
            dimension_semantics=("arbitrary", "arbitrary"), vmem_limit_bytes=VMEM_LIMIT_BYTES),
        name="ada_table",
    )(cc, w_ada, b_ada.reshape(depth, 1, N_MOD * D_MODEL))


def _fir(src_ref, dst_ref, w_ref, b_ref, *, n_taps, src_off, n_out, width, block, unrolled):
    for c in range(width // LANES):
        lanes = slice(c * LANES, (c + 1) * LANES)
        wk = [jnp.broadcast_to(w_ref[k:k + 1, lanes], (BATCH, LANES)) for k in range(n_taps)]
        bias = jnp.broadcast_to(b_ref[:, lanes], (BATCH, LANES))

        def one_block(base):
            acc = [bias] * block
            for q in range(block + n_taps - 1):
                z = src_ref[pl.ds(base + src_off + q * BATCH, BATCH), lanes]
                for j in range(max(0, q - n_taps + 1), min(block, q + 1)):
                    acc[j] = acc[j] + wk[q - j] * z
            for j in range(block):
                dst_ref[pl.ds(base + j * BATCH, BATCH), lanes] = acc[j]

        if unrolled:
            for blk in range(n_out // block):
                one_block(blk * block * BATCH)
        else:
            def body(blk, carry):
                one_block(pl.multiple_of(blk * (block * BATCH), block * BATCH))
                return carry

            lax.fori_loop(0, n_out // block, body, 0, unroll=2)


SCAN_RB = 128
LOG2_E = 1.4426950408889634


def _rglru(u_ref, ub_ref, lam_ref, wab_ref, bab_ref, a_scr, b_scr, h_scr, reverse):
    lam = lam_ref[...]
    softplus = jnp.maximum(-lam, 0.0) + jnp.log1p(jnp.exp(-jnp.abs(lam)))
    k = (-0.5 * RG_C * LOG2_E) * softplus
    n_rb = ROWS // SCAN_RB
    h = h_scr[...]
    for rb in (reversed(range(n_rb)) if reverse else range(n_rb)):
        rows = slice(rb * SCAN_RB, (rb + 1) * SCAN_RB)
        for j in range(N_BD):
            cols = slice(j * MXU_N, (j + 1) * MXU_N)
            pre = jnp.dot(ub_ref[rows, BD_K0[j]:BD_K0[j] + BD_WIN], wab_ref[j],
                          preferred_element_type=F32)
            u = u_ref[rows, cols] if u_ref is not None else ub_ref[rows, cols].astype(F32)
            t_r = jnp.tanh(pre[:, :MXU_N] + bab_ref[0:1, cols])
            t_i = jnp.tanh(pre[:, MXU_N:] + bab_ref[1:2, cols])
            a = jnp.exp2(k[:, cols] * t_r + k[:, cols])
            w = 1.0 - a * a
            sq = jnp.where(w > 0.0, w * lax.rsqrt(w), 0.0)
            a_scr[rows, cols] = a
            b_scr[rows, cols] = (0.5 * sq) * ((t_i + 1.0) * u)
        steps = range(rb * SCAN_RB // BATCH, (rb + 1) * SCAN_RB // BATCH)
        for t in (reversed(steps) if reverse else steps):
            r8 = slice(t * BATCH, (t + 1) * BATCH)
            h = a_scr[r8, :] * h + b_scr[r8, :]
            b_scr[r8, :] = h
    h_scr[...] = h
    return h


def _proj_kernel(*refs, nchunks, add_pos, want_gate, n_cast):
    it = iter(refs)
    xm_ref, xp_ref, xn_ref = next(it), next(it), next(it)
    if add_pos:
        rowtab_ref, coltab_ref = next(it), next(it)
    g_ref, sh_ref, sc_ref = next(it), next(it), next(it)
    wr_ref, cw_ref, cb_ref = next(it), next(it), next(it)
    h0_ref, lam_ref, wab_ref, bab_ref = next(it), next(it), next(it), next(it)
    if want_gate:
        wg_ref = next(it)
    cast_in = [next(it) for _ in range(n_cast)]
    if add_pos:
        x0_ref = next(it)
    u_ref = next(it)
    if want_gate:
        gate_ref = next(it)
        yf_ref = next(it)
    hT_ref = next(it)
    cast_out = [next(it) for _ in range(n_cast)]
    zr_scr, u_scr, a_scr, b_scr, h_scr = next(it), next(it), next(it), next(it), next(it)

    for src, dst in zip(cast_in, cast_out):
        dst[...] = src[...].astype(BF16)

    i = pl.program_id(0)

    @pl.when(i == 0)
    def _():
        h_scr[...] = h0_ref[...]

    xm = jnp.swapaxes(xm_ref[...], 0, 1)
    xp = jnp.swapaxes(xp_ref[...], 0, 1)[XHALO_T - HALO_T:]
    xn = jnp.swapaxes(xn_ref[...], 0, 1)[:HALO_T]
    x3 = jnp.concatenate([xp, xm, xn], axis=0)
    if add_pos:
        half = D_MODEL // 2
        r_prev = rowtab_ref[pl.ds(jnp.maximum(i - 1, 0), 1)]
        r_here = rowtab_ref[pl.ds(i, 1)]
        r_next = rowtab_ref[pl.ds(jnp.minimum(i + 1, nchunks - 1), 1)]
        prow = jnp.concatenate([jnp.broadcast_to(r_prev, (HALO_T, 1, half)),
                                jnp.broadcast_to(r_here, (TT, 1, half)),
                                jnp.broadcast_to(r_next, (HALO_T, 1, half))], axis=0)
        pcol = jnp.concatenate([coltab_ref[GRID_W - HALO_T:], coltab_ref[...], coltab_ref[:HALO_T]], axis=0)
        x3 = x3 + jnp.concatenate([prow, pcol], axis=-1)
        x0_ref[...] = x3[HALO_T:HALO_T + TT].reshape(ROWS, D_MODEL)
    h = _norm_mod(x3, g_ref[...], sh_ref[...], sc_ref[...])
    keep_prev = jnp.where(i == 0, 0.0, 1.0)
    keep_next = jnp.where(i == nchunks - 1, 0.0, 1.0)
    h = jnp.concatenate([h[:HALO_T] * keep_prev, h[HALO_T:HALO_T + TT], h[HALO_T + TT:] * keep_next], axis=0)
    h = h.reshape(ROWS + 2 * HALO, D_MODEL).astype(BF16)

    zr_scr[...] = jnp.dot(h, wr_ref[...], preferred_element_type=F32)
    if want_gate:
        zg = jnp.dot(h[HALO:HALO + ROWS], wg_ref[...], preferred_element_type=F32)
        gate_ref[...] = jax.nn.gelu(zg).astype(BF16)

    _fir(zr_scr, u_scr, cw_ref, cb_ref, n_taps=REC_CONV_W, src_off=HALO - BATCH, n_out=TT, width=D_RNN, block=16,
         unrolled=True)
    u_ref[...] = u_scr[...].astype(BF16)

    hT_ref[...] = _rglru(u_scr, u_ref, lam_ref, wab_ref, bab_ref, a_scr, b_scr, h_scr, reverse=False)
    if want_gate:
        yf_ref[...] = b_scr[...].astype(BF16)


def _rec_project(x, pos_tabs, mods, ctx, norm_g, w_in, conv_w, conv_b, h0, lam, wab, bab, want_gate, cast=()):
    t_len = x.shape[1]
    n = t_len * BATCH
    nchunks = t_len // TT
    add_pos = pos_tabs is not None
    hb = TT // XHALO_T
    nhb = t_len // XHALO_T

    in_specs = [
        pl.BlockSpec((BATCH, TT, D_MODEL), lambda i: (0, i, 0)),
        pl.BlockSpec((BATCH, XHALO_T, D_MODEL), lambda i: (0, jnp.maximum(i * hb - 1, 0), 0)),
        pl.BlockSpec((BATCH, XHALO_T, D_MODEL), lambda i: (0, jnp.minimum((i + 1) * hb, nhb - 1), 0)),
    ]
    args = [x, x, x]
    if add_pos:
        assert TT == GRID_W and pos_tabs[0].shape[0] == nchunks
        in_specs += [_const_spec(pos_tabs[0].shape), _const_spec(pos_tabs[1].shape)]
        args += list(pos_tabs)
    in_specs += [
        _layer_spec((1, D_MODEL), 0, 0, 0),
        _mod_spec(0, ctx, 0),
        _mod_spec(0, ctx, 1),
        _layer_spec((D_MODEL, D_RNN), 0, 0, 1),
        _layer_spec((REC_CONV_W, D_RNN), 0, 0, 0),
        _layer_spec((1, D_RNN), 0, 0, 0),
    ]
    args += [norm_g, mods, mods, w_in, conv_w, conv_b]
    in_specs += [
        _const_spec((BATCH, D_RNN)),
        _layer_spec((1, D_RNN), 0, 0, 0),
        _layer_spec((N_BD, BD_WIN, 2 * MXU_N), 0, 0, 0, 0),
        _layer_spec((2, D_RNN), 0, 0, 0),
    ]
    args += [h0, lam, wab, bab]
    if want_gate:
        in_specs.append(_layer_spec((D_MODEL, D_RNN), 0, 0, 0))
        args.append(w_in)
    cast_specs = []
    for w in cast:
        slab = (w.shape[0], w.shape[1] // nchunks, w.shape[2])
        assert w.shape[1] % nchunks == 0 and slab[1] % 16 == 0, w.shape
        cast_specs.append(pl.BlockSpec(slab, lambda i: (0, i, 0)))
    in_specs += cast_specs
    args += list(cast)

    out_specs, out_shape = [], []
    if add_pos:
        out_specs.append(pl.BlockSpec((ROWS, D_MODEL), lambda i: (i, 0)))
        out_shape.append(jax.ShapeDtypeStruct((n, D_MODEL), F32))
    out_specs.append(pl.BlockSpec((ROWS, D_RNN), lambda i: (i, 0)))
    out_shape.append(jax.ShapeDtypeStruct((n, D_RNN), BF16))
    if want_gate:
        out_specs.append(pl.BlockSpec((ROWS, D_RNN), lambda i: (i, 0)))
        out_shape.append(jax.ShapeDtypeStruct((n, D_RNN), BF16))
        out_specs.append(pl.BlockSpec((ROWS, D_RNN), lambda i: (i, 0)))
        out_shape.append(jax.ShapeDtypeStruct((n, D_RNN), BF16))
    out_specs.append(pl.BlockSpec((BATCH, D_RNN), lambda i: (0, 0)))
    out_shape.append(jax.ShapeDtypeStruct((BATCH, D_RNN), F32))
    out_specs += cast_specs
    out_shape += [jax.ShapeDtypeStruct(w.shape, BF16) for w in cast]

    return pl.pallas_call(
        functools.partial(_proj_kernel, nchunks=nchunks, add_pos=add_pos, want_gate=want_gate, n_cast=len(cast)),
        grid=(nchunks,),
        in_specs=in_specs,
        out_specs=out_specs,
        out_shape=out_shape,
        scratch_shapes=[
            pltpu.VMEM((ROWS + 2 * HALO, D_RNN), F32),
            pltpu.VMEM((ROWS, D_RNN), F32),
            pltpu.VMEM((ROWS, D_RNN), F32),
            pltpu.VMEM((ROWS, D_RNN), F32),
            pltpu.VMEM((BATCH, D_RNN), F32),
        ],
        compiler_params=_cparams(),
        name="rec_fwd_lat" if add_pos else "rec_fwd_ctx",
    )(*args)


def _mlp_block(x3, g, sh, sc, gt, w1_ref, w2_ref):
    h = _norm_mod(x3, g, sh, sc).reshape(ROWS, D_MODEL).astype(BF16)
    m = jnp.dot(h, w1_ref[...], preferred_element_type=F32)
    m = jnp.square(jnp.maximum(m, 0.0)).astype(BF16)
    o = jnp.dot(m, w2_ref[...], preferred_element_type=F32)
    return x3 + gt * o.reshape(TT, BATCH, D_MODEL)


def _scan_kernel(*refs, mode):
    it = iter(refs)
    u_ref, h0_ref, lam_ref, wab_ref, bab_ref = next(it), next(it), next(it), next(it), next(it)
    if mode == "out":
        yf_ref, gate_ref, x0_ref, g1_ref, wo_ref = next(it), next(it), next(it), next(it), next(it)
        ng_ref, sh2_ref, sc2_ref, g2_ref, w1_ref, w2_ref = (next(it) for _ in range(6))
        cg_ref, csh_ref, csc_ref, wp_ref, bp_ref = (next(it) for _ in range(5))
        x2_ref, z_ref = next(it), next(it)
    hT_ref = next(it)
    a_scr, b_scr, h_scr = next(it), next(it), next(it)

    i = pl.program_id(0)

    @pl.when(i == 0)
    def _():
        h_scr[...] = h0_ref[...]

    hT_ref[...] = _rglru(None, u_ref, lam_ref, wab_ref, bab_ref, a_scr, b_scr, h_scr, reverse=True)

    if mode == "out":
        y = b_scr[...] + yf_ref[...].astype(F32)
        gy = (gate_ref[...].astype(F32) * y).astype(BF16)
        o = jnp.dot(gy, wo_ref[...], preferred_element_type=F32)
        x1 = x0_ref[...].reshape(TT, BATCH, D_MODEL) + g1_ref[...] * o.reshape(TT, BATCH, D_MODEL)
        x2 = _mlp_block(x1, ng_ref[...], sh2_ref[...], sc2_ref[...], g2_ref[...], w1_ref, w2_ref)
        x2_ref[...] = x2.reshape(ROWS, D_MODEL)
        hc = _norm_mod(x2, cg_ref[...], csh_ref[...], csc_ref[...]).reshape(ROWS, D_MODEL).astype(BF16)
        zz = jnp.dot(hc, wp_ref[...], preferred_element_type=F32) + bp_ref[...]
        z_ref[...] = (zz[:, :D_MODEL] * _sigmoid(zz[:, D_MODEL:])).astype(BF16)


def _rec_scan_rev(u, h0, lam, wab, bab, *, mode, extra=None, mods=None):
    n = u.shape[0]
    nchunks = n // ROWS
    order = lambda i: (nchunks - 1 - i, 0)
    in_specs = [
        pl.BlockSpec((ROWS, D_RNN), order),
        _const_spec((BATCH, D_RNN)),
        _layer_spec((1, D_RNN), 1, 0, 0),
        _layer_spec((N_BD, BD_WIN, 2 * MXU_N), 1, 0, 0, 0),
        _layer_spec((2, D_RNN), 1, 0, 0),
    ]
    args = [u, h0, lam, wab, bab]
    out_specs, out_shape = [], []
    if mode == "out":
        yf, gate, x0, w_out, norm_g, w1, w2, w_pw1, b_pw1 = extra
        in_specs += [
            pl.BlockSpec((ROWS, D_RNN), order),
            pl.BlockSpec((ROWS, D_RNN), order),
            pl.BlockSpec((ROWS, D_MODEL), order),
            _mod_spec(0, False, 2),
            _layer_spec((D_RNN, D_MODEL), 0, 0, 0),
            _layer_spec((1, D_MODEL), 1, 0, 0),
            _mod_spec(0, False, 3),
            _mod_spec(0, False, 4),
            _mod_spec(0, False, 5),
            _layer_spec((D_MODEL, D_FF), 0, 0, 0),
            _layer_spec((D_FF, D_MODEL), 0, 0, 0),
            _layer_spec((1, D_MODEL), 2, 0, 0),
            _mod_spec(1, False, 0),
            _mod_spec(1, False, 1),
            _layer_spec((D_MODEL, 2 * D_MODEL), 0, 0, 0),
            _const_spec((1, 2 * D_MODEL)),
        ]
        args += [yf, gate, x0, mods, w_out, norm_g, mods, mods, mods, w1, w2, norm_g, mods, mods, w_pw1, b_pw1]
        out_specs.append(pl.BlockSpec((ROWS, D_MODEL), order))
        out_shape.append(jax.ShapeDtypeStruct((n, D_MODEL), F32))
        out_specs.append(pl.BlockSpec((ROWS, D_MODEL), order))
        out_shape.append(jax.ShapeDtypeStruct((n, D_MODEL), BF16))
    out_specs.append(pl.BlockSpec((BATCH, D_RNN), lambda i: (0, 0)))
    out_shape.append(jax.ShapeDtypeStruct((BATCH, D_RNN), F32))

    return pl.pallas_call(
        functools.partial(_scan_kernel, mode=mode),
        grid=(nchunks,),
        in_specs=in_specs,
        out_specs=out_specs,
        out_shape=out_shape,
        scratch_shapes=[
            pltpu.VMEM((ROWS, D_RNN), F32),
            pltpu.VMEM((ROWS, D_RNN), F32),
            pltpu.VMEM((BATCH, D_RNN), F32),
        ],
        compiler_params=_cparams(),
        name="rec_rev_" + mode,
    )(*args)


def _conf_out_kernel(zm_ref, zp_ref, zn_ref, x_ref, cw_ref, cb_ref, lg_ref, lb_ref, w_ref, b_ref, g1_ref,
                     ng_ref, sh2_ref, sc2_ref, g2_ref, w1_ref, w2_ref, fg_ref,
                     o_ref, zext_scr, conv_scr, *, nchunks):
    i = pl.program_id(0)
    zext_scr[0:CONF_HALO, :] = zp_ref[...].astype(F32)
    zext_scr[CONF_HALO:CONF_HALO + ROWS, :] = zm_ref[...].astype(F32)
    zext_scr[CONF_HALO + ROWS:, :] = zn_ref[...].astype(F32)

    @pl.when(i == 0)
    def _():
        zext_scr[0:CONF_HALO, :] = jnp.zeros((CONF_HALO, D_MODEL), F32)

    @pl.when(i == nchunks - 1)
    def _():
        zext_scr[CONF_HALO + ROWS:, :] = jnp.zeros((CONF_HALO, D_MODEL), F32)

    _fir(zext_scr, conv_scr, cw_ref, cb_ref, n_taps=CONF_KW, src_off=CONF_HALO - CONF_HALF * BATCH,
         n_out=TT, width=D_MODEL, block=16, unrolled=False)

    z = conv_scr[...]
    mu = jnp.mean(z, axis=-1, keepdims=True)
    zc = z - mu
    var = jnp.mean(zc * zc, axis=-1, keepdims=True)
    zl = zc * lax.rsqrt(var + EPS) * lg_ref[...] + lb_ref[...]
    zs = (zl * _sigmoid(zl)).astype(BF16)
    y = jnp.dot(zs, w_ref[...], preferred_element_type=F32) + b_ref[...]
    x3 = x_ref[...].reshape(TT, BATCH, D_MODEL) + g1_ref[...] * y.reshape(TT, BATCH, D_MODEL)
    x4 = _mlp_block(x3, ng_ref[...], sh2_ref[...], sc2_ref[...], g2_ref[...], w1_ref, w2_ref)
    o_ref[...] = jnp.swapaxes(_rms(x4, fg_ref[...]), 0, 1)


def _conf_out(z, x, mods, layer, norm_g, conv_w, conv_b, ln_g, ln_b, w_pw2, b_pw2, w1, w2, final_g):
    n = x.shape[0]
    nchunks = n // ROWS
    hb = ROWS // CONF_HALO
    return pl.pallas_call(
        functools.partial(_conf_out_kernel, nchunks=nchunks),
        grid=(nchunks,),
        in_specs=[
            pl.BlockSpec((ROWS, D_MODEL), lambda i: (i, 0)),
            pl.BlockSpec((CONF_HALO, D_MODEL), lambda i: (jnp.maximum(i * hb - 1, 0), 0)),
            pl.BlockSpec((CONF_HALO, D_MODEL), lambda i: (jnp.minimum((i + 1) * hb, n // CONF_HALO - 1), 0)),
            pl.BlockSpec((ROWS, D_MODEL), lambda i: (i, 0)),
            _layer_spec((CONF_KW, D_MODEL), 0, 0, 0),
            _const_spec((1, D_MODEL)),
            _const_spec((1, D_MODEL)),
            _const_spec((1, D_MODEL)),
            _layer_spec((D_MODEL, D_MODEL), 0, 0, 0),
            _const_spec((1, D_MODEL)),
            _mod_spec(layer, False, 2),
            _layer_spec((1, D_MODEL), 2 * layer + 1, 0, 0),
            _mod_spec(layer, False, 3),
            _mod_spec(layer, False, 4),
            _mod_spec(layer, False, 5),
            _layer_spec((D_MODEL, D_FF), layer, 0, 0),
            _layer_spec((D_FF, D_MODEL), layer, 0, 0),
            _const_spec((1, D_MODEL)),
        ],
        out_specs=pl.BlockSpec((BATCH, TT, D_MODEL), lambda i: (0, i, 0)),
        out_shape=jax.ShapeDtypeStruct((BATCH, n // BATCH, D_MODEL), F32),
        scratch_shapes=[
            pltpu.VMEM((ROWS + 2 * CONF_HALO, D_MODEL), F32),
            pltpu.VMEM((ROWS, D_MODEL), F32),
        ],
        compiler_params=_cparams(),
        name="conf_out",
    )(z, z, z, x, conv_w, conv_b, ln_g, ln_b, w_pw2, b_pw2, mods, norm_g, mods, mods, mods, w1, w2, final_g)


def _pos_tables(rows, d):
    q = d // 4
    omega = 1.0 / (POS_BASE ** (jnp.arange(q, dtype=F32) / q))
    er = jnp.arange(rows, dtype=jnp.int32).astype(F32)[:, None] * omega[None, :]
    ec = jnp.arange(GRID_W, dtype=jnp.int32).astype(F32)[:, None] * omega[None, :]
    rowtab = jnp.concatenate([jnp.sin(er), jnp.cos(er)], axis=-1)
    coltab = jnp.concatenate([jnp.sin(ec), jnp.cos(ec)], axis=-1)
    return rowtab.reshape(rows, 1, d // 2), coltab.reshape(GRID_W, 1, d // 2)


def _block_diag_windows(w_a, w_x):
    w = jnp.stack([w_a, w_x], axis=1).reshape(2, 2, D_RNN, RNN_BLOCK)
    tiled_eye = np.tile(np.eye(RNN_BLOCK, dtype=np.float32), (1, N_RNN_BLOCKS))
    row_blk = np.arange(D_RNN)[:, None] // RNN_BLOCK
    col_blk = np.arange(D_RNN)[None, :] // RNN_BLOCK
    w_win, sel, msk = [], [], []
    for j, k0 in enumerate(BD_K0):
        cols = slice(j * MXU_N, (j + 1) * MXU_N)
        w_win.append(w[:, :, k0:k0 + BD_WIN])
        sel.append(tiled_eye[:, cols])
        msk.append(0.5 * (row_blk[k0:k0 + BD_WIN] == col_blk[:, cols]))
    w_win = jnp.stack(w_win, axis=2)
    sel = jnp.asarray(np.stack(sel))
    msk = jnp.asarray(np.stack(msk), dtype=F32)
    full = jnp.einsum("dgjrk,jkc->djrgc", w_win, sel) * msk[None, :, :, None, :]
    return full.reshape(2, N_BD, BD_WIN, 2 * MXU_N).astype(BF16)


def kernel(x, c, ctx, c_ctx, w_ada, b_ada, norm_g, rec_w_in, rec_conv_w, rec_conv_b, rec_lambda, rec_w_a, rec_b_a,
           rec_w_x, rec_b_x, rec_w_out, conf_w_pw1, conf_b_pw1, conf_conv_w, conf_conv_b, conf_ln_g, conf_ln_b,
           conf_w_pw2, conf_b_pw2, mlp_w_in, mlp_w_out, final_g):
    assert x.shape == (BATCH, SEQ, D_MODEL) and ctx.shape == (BATCH, CTX_LEN, D_MODEL)
    assert w_ada.shape[0] == 2, "one recurrent layer followed by one conformer layer"

    cc = jnp.concatenate([c, jnp.broadcast_to(c_ctx[None, :], (BATCH, D_MODEL))], axis=0)
    mods = _ada_table(cc, w_ada, b_ada)
    norm_g = norm_g.reshape(4, 1, D_MODEL)

    w_in = rec_w_in.astype(BF16)
    conv_b = rec_conv_b.reshape(1, 1, D_RNN)
    wab = _block_diag_windows(rec_w_a[0], rec_w_x[0])
    bab = 0.5 * jnp.stack([rec_b_a[0].reshape(2, D_RNN), rec_b_x[0].reshape(2, D_RNN)], axis=1)
    lam = rec_lambda[0].reshape(2, 1, D_RNN)
    zeros = jnp.zeros((BATCH, D_RNN), F32)
    rec = (lam, wab, bab)

    u_ctx, h0_f = _rec_project(ctx, None, mods, True, norm_g, w_in, rec_conv_w, conv_b, zeros, *rec, False)
    (h0_r,) = _rec_scan_rev(u_ctx, zeros, *rec, mode="state")

    pos_tabs = _pos_tables(SEQ // GRID_W, D_MODEL)
    x0, u_lat, gate, y_f, _, mlp_w1, mlp_w2, w_pw1, w_pw2 = _rec_project(
        x, pos_tabs, mods, False, norm_g, w_in, rec_conv_w, conv_b, h0_f, *rec, True,
        cast=(mlp_w_in, mlp_w_out, conf_w_pw1, conf_w_pw2))
    x2, z, _ = _rec_scan_rev(u_lat, h0_r, *rec, mode="out", mods=mods,
                             extra=(y_f, gate, x0, rec_w_out.astype(BF16), norm_g, mlp_w1, mlp_w2,
                                    w_pw1, conf_b_pw1.reshape(1, 2 * D_MODEL)))

    return _conf_out(z, x2, mods, 1, norm_g, conf_conv_w, conf_conv_b.reshape(1, D_MODEL),
                     conf_ln_g.reshape(1, D_MODEL), conf_ln_b.reshape(1, D_MODEL),
                     w_pw2, conf_b_pw2.reshape(1, D_MODEL), mlp_w1, mlp_w2, final_g.reshape(1, D_MODEL))
```

```python
import functools

import jax
import jax.numpy as jnp
import numpy as np
from jax import lax
from jax.experimental import pallas as pl
from jax.experimental.pallas import tpu as pltpu

F32 = jnp.float32
BF16 = jnp.bfloat16

D_MODEL = 1024
BATCH = 8
SEQ = 2048
CTX_LEN = 256
GRID_W = 64
D_RNN = 1280
N_RNN_BLOCKS = 16
RNN_BLOCK = D_RNN // N_RNN_BLOCKS
REC_CONV_W = 4
RG_C = 8.0
CONF_KW = 31
CONF_HALF = CONF_KW // 2
D_FF = 4 * D_MODEL
N_MOD = 6
EPS = 1e-6
POS_BASE = 10000.0

VMEM_LIMIT_BYTES = 56 * 1024 * 1024
LANES = 128
MXU_N = 256
BD_WIN = 512
BD_K0 = (0, 128, 384, 640, 768)
N_BD = D_RNN // MXU_N

TT = 64
ROWS = TT * BATCH
HALO = 16
HALO_T = HALO // BATCH
XHALO_T = 8
CONF_HALO = 128


def _cparams():
    return pltpu.CompilerParams(dimension_semantics=("arbitrary",), vmem_limit_bytes=VMEM_LIMIT_BYTES)


def _const_spec(shape):
    return pl.BlockSpec(shape, lambda *_: (0,) * len(shape), pipeline_mode=pl.Buffered(1))


def _layer_spec(shape, *idx):
    return pl.BlockSpec((None,) + tuple(shape), lambda *_: tuple(idx), pipeline_mode=pl.Buffered(1))


def _mod_spec(layer, ctx, k):
    return _layer_spec((BATCH, D_MODEL), layer, int(ctx), k)


def _sigmoid(x):
    return 0.5 * jnp.tanh(0.5 * x) + 0.5


def _rms(x3, g):
    ms = jnp.mean(x3 * x3, axis=-1, keepdims=True)
    return x3 * lax.rsqrt(ms + EPS) * g


def _norm_mod(x3, g, sh, sc):
    return _rms(x3, g) * (1.0 + sc) + sh


def _ada_kernel(c_ref, w_ref, b_ref, o_ref):
    c = c_ref[...]
    s = (c * jax.nn.sigmoid(c)).astype(BF16)
    o_ref[...] = jnp.dot(s, w_ref[...].astype(BF16), preferred_element_type=F32) + b_ref[...]


def _ada_table(cc, w_ada, b_ada):
    depth = w_ada.shape[0]
    tn = 1536
    return pl.pallas_call(
        _ada_kernel,
        grid=(depth, N_MOD * D_MODEL // tn),
        in_specs=[
            pl.BlockSpec((2 * BATCH, D_MODEL), lambda l, j: (0, 0)),
            pl.BlockSpec((None, D_MODEL, tn), lambda l, j: (l, 0, j)),
            pl.BlockSpec((None, 1, tn), lambda l, j: (l, 0, j)),
        ],
        out_specs=pl.BlockSpec((None, 2 * BATCH, tn), lambda l, j: (l, 0, j)),
        out_shape=jax.ShapeDtypeStruct((depth, 2 * BATCH, N_MOD * D_MODEL), F32),
        compiler_params=pltpu.CompilerParams(
            dimension_semantics=("arbitrary", "arbitrary"), vmem_limit_bytes=VMEM_LIMIT_BYTES),
        name="ada_table",
    )(cc, w_ada, b_ada.reshape(depth, 1, N_MOD * D_MODEL))


def _fir(src_ref, dst_ref, w_ref, b_ref, *, n_taps, src_off, n_out, width, block, unrolled):
    for c in range(width // LANES):
        lanes = slice(c * LANES, (c + 1) * LANES)
        wk = [jnp.broadcast_to(w_ref[k:k + 1, lanes], (BATCH, LANES)) for k in range(n_taps)]
        bias = jnp.broadcast_to(b_ref[:, lanes], (BATCH, LANES))

        def one_block(base):
            acc = [bias] * block
            for q in range(block + n_taps - 1):
                z = src_ref[pl.ds(base + src_off + q * BATCH, BATCH), lanes]
                for j in range(max(0, q - n_taps + 1), min(block, q + 1)):
                    acc[j] = acc[j] + wk[q - j] * z
            for j in range(block):
                dst_ref[pl.ds(base + j * BATCH, BATCH), lanes] = acc[j]

        if unrolled:
            for blk in range(n_out // block):
                one_block(blk * block * BATCH)
        else:
            def body(blk, carry):
                one_block(pl.multiple_of(blk * (block * BATCH), block * BATCH))
                return carry

            lax.fori_loop(0, n_out // block, body, 0, unroll=2)


SCAN_RB = 128
LOG2_E = 1.4426950408889634


def _rglru(u_ref, ub_ref, lam_ref, wab_ref, bab_ref, a_scr, b_scr, h_scr, reverse):
    lam = lam_ref[...]
    softplus = jnp.maximum(-lam, 0.0) + jnp.log1p(jnp.exp(-jnp.abs(lam)))
    k = (-0.5 * RG_C * LOG2_E) * softplus
    n_rb = ROWS // SCAN_RB
    h = h_scr[...]
    for rb in (reversed(range(n_rb)) if reverse else range(n_rb)):
        rows = slice(rb * SCAN_RB, (rb + 1) * SCAN_RB)
        for j in range(N_BD):
            cols = slice(j * MXU_N, (j + 1) * MXU_N)
            pre = jnp.dot(ub_ref[rows, BD_K0[j]:BD_K0[j] + BD_WIN], wab_ref[j],
                          preferred_element_type=F32)
            u = u_ref[rows, cols] if u_ref is not None else ub_ref[rows, cols].astype(F32)
            t_r = jnp.tanh(pre[:, :MXU_N] + bab_ref[0:1, cols])
            t_i = jnp.tanh(pre[:, MXU_N:] + bab_ref[1:2, cols])
            a = jnp.exp2(k[:, cols] * t_r + k[:, cols])
            w = 1.0 - a * a
            sq = jnp.where(w > 0.0, w * lax.rsqrt(w), 0.0)
            a_scr[rows, cols] = a
            b_scr[rows, cols] = (0.5 * sq) * ((t_i + 1.0) * u)
        steps = range(rb * SCAN_RB // BATCH, (rb + 1) * SCAN_RB // BATCH)
        for t in (reversed(steps) if reverse else steps):
            r8 = slice(t * BATCH, (t + 1) * BATCH)
            h = a_scr[r8, :] * h + b_scr[r8, :]
            b_scr[r8, :] = h
    h_scr[...] = h
    return h


def _proj_kernel(*refs, nchunks, add_pos, want_gate, n_cast):
    it = iter(refs)
    xm_ref, xp_ref, xn_ref = next(it), next(it), next(it)
    if add_pos:
        rowtab_ref, coltab_ref = next(it), next(it)
    g_ref, sh_ref, sc_ref = next(it), next(it), next(it)
    wr_ref, cw_ref, cb_ref = next(it), next(it), next(it)
    h0_ref, lam_ref, wab_ref, bab_ref = next(it), next(it), next(it), next(it)
    if want_gate:
        wg_ref = next(it)
    cast_in = [next(it) for _ in range(n_cast)]
    if add_pos:
        x0_ref = next(it)
    u_ref = next(it)
    if want_gate:
        gate_ref = next(it)
        yf_ref = next(it)
    hT_ref = next(it)
    cast_out = [next(it) for _ in range(n_cast)]
    zr_scr, u_scr, a_scr, b_scr, h_scr = next(it), next(it), next(it), next(it), next(it)

    for src, dst in zip(cast_in, cast_out):
        dst[...] = src[...].astype(BF16)

    i = pl.program_id(0)

    @pl.when(i == 0)
    def _():
        h_scr[...] = h0_ref[...]

    xm = jnp.swapaxes(xm_ref[...], 0, 1)
    xp = jnp.swapaxes(xp_ref[...], 0, 1)[XHALO_T - HALO_T:]
    xn = jnp.swapaxes(xn_ref[...], 0, 1)[:HALO_T]
    x3 = jnp.concatenate([xp, xm, xn], axis=0)
    if add_pos:
        half = D_MODEL // 2
        r_prev = rowtab_ref[pl.ds(jnp.maximum(i - 1, 0), 1)]
        r_here = rowtab_ref[pl.ds(i, 1)]
        r_next = rowtab_ref[pl.ds(jnp.minimum(i + 1, nchunks - 1), 1)]
        prow = jnp.concatenate([jnp.broadcast_to(r_prev, (HALO_T, 1, half)),
                                jnp.broadcast_to(r_here, (TT, 1, half)),
                                jnp.broadcast_to(r_next, (HALO_T, 1, half))], axis=0)
        pcol = jnp.concatenate([coltab_ref[GRID_W - HALO_T:], coltab_ref[...], coltab_ref[:HALO_T]], axis=0)
        x3 = x3 + jnp.concatenate([prow, pcol], axis=-1)
        x0_ref[...] = x3[HALO_T:HALO_T + TT].reshape(ROWS, D_MODEL)
    h = _norm_mod(x3, g_ref[...], sh_ref[...], sc_ref[...])
    keep_prev = jnp.where(i == 0, 0.0, 1.0)
    keep_next = jnp.where(i == nchunks - 1, 0.0, 1.0)
    h = jnp.concatenate([h[:HALO_T] * keep_prev, h[HALO_T:HALO_T + TT], h[HALO_T + TT:] * keep_next], axis=0)
    h = h.reshape(ROWS + 2 * HALO, D_MODEL).astype(BF16)

    zr_scr[...] = jnp.dot(h, wr_ref[...], preferred_element_type=F32)
    if want_gate:
        zg = jnp.dot(h[HALO:HALO + ROWS], wg_ref[...], preferred_element_type=F32)
        gate_ref[...] = jax.nn.gelu(zg).astype(BF16)

    _fir(zr_scr, u_scr, cw_ref, cb_ref, n_taps=REC_CONV_W, src_off=HALO - BATCH, n_out=TT, width=D_RNN, block=16,
         unrolled=True)
    u_ref[...] = u_scr[...].astype(BF16)

    hT_ref[...] = _rglru(u_scr, u_ref, lam_ref, wab_ref, bab_ref, a_scr, b_scr, h_scr, reverse=False)
    if want_gate:
        yf_ref[...] = b_scr[...].astype(BF16)


def _rec_project(x, pos_tabs, mods, ctx, norm_g, w_in, conv_w, conv_b, h0, lam, wab, bab, want_gate, cast=()):
    t_len = x.shape[1]
    n = t_len * BATCH
    nchunks = t_len // TT
    add_pos = pos_tabs is not None
    hb = TT // XHALO_T
    nhb = t_len // XHALO_T

    in_specs = [
        pl.BlockSpec((BATCH, TT, D_MODEL), lambda i: (0, i, 0)),
        pl.BlockSpec((BATCH, XHALO_T, D_MODEL), lambda i: (0, jnp.maximum(i * hb - 1, 0), 0)),
        pl.BlockSpec((BATCH, XHALO_T, D_MODEL), lambda i: (0, jnp.minimum((i + 1) * hb, nhb - 1), 0)),
    ]
    args = [x, x, x]
    if add_pos:
        assert TT == GRID_W and pos_tabs[0].shape[0] == nchunks
        in_specs += [_const_spec(pos_tabs[0].shape), _const_spec(pos_tabs[1].shape)]
        args += list(pos_tabs)
    in_specs += [
        _layer_spec((1, D_MODEL), 0, 0, 0),
        _mod_spec(0, ctx, 0),
        _mod_spec(0, ctx, 1),
        _layer_spec((D_MODEL, D_RNN), 0, 0, 1),
        _layer_spec((REC_CONV_W, D_RNN), 0, 0, 0),
        _layer_spec((1, D_RNN), 0, 0, 0),
    ]
    args += [norm_g, mods, mods, w_in, conv_w, conv_b]
    in_specs += [
        _const_spec((BATCH, D_RNN)),
        _layer_spec((1, D_RNN), 0, 0, 0),
        _layer_spec((N_BD, BD_WIN, 2 * MXU_N), 0, 0, 0, 0),
        _layer_spec((2, D_RNN), 0, 0, 0),
    ]
    args += [h0, lam, wab, bab]
    if want_gate:
        in_specs.append(_layer_spec((D_MODEL, D_RNN), 0, 0, 0))
        args.append(w_in)
    cast_specs = []
    for w in cast:
        slab = (w.shape[0], w.shape[1] // nchunks, w.shape[2])
        assert w.shape[1] % nchunks == 0 and slab[1] % 16 == 0, w.shape
        cast_specs.append(pl.BlockSpec(slab, lambda i: (0, i, 0)))
    in_specs += cast_specs
    args += list(cast)

    out_specs, out_shape = [], []
    if add_pos:
        out_specs.append(pl.BlockSpec((ROWS, D_MODEL), lambda i: (i, 0)))
        out_shape.append(jax.ShapeDtypeStruct((n, D_MODEL), F32))
    out_specs.append(pl.BlockSpec((ROWS, D_RNN), lambda i: (i, 0)))
    out_shape.append(jax.ShapeDtypeStruct((n, D_RNN), BF16))
    if want_gate:
        out_specs.append(pl.BlockSpec((ROWS, D_RNN), lambda i: (i, 0)))
        out_shape.append(jax.ShapeDtypeStruct((n, D_RNN), BF16))
        out_specs.append(pl.BlockSpec((ROWS, D_RNN), lambda i: (i, 0)))
        out_shape.append(jax.ShapeDtypeStruct((n, D_RNN), BF16))
    out_specs.append(pl.BlockSpec((BATCH, D_RNN), lambda i: (0, 0)))
    out_shape.append(jax.ShapeDtypeStruct((BATCH, D_RNN), F32))
    out_specs += cast_specs
    out_shape += [jax.ShapeDtypeStruct(w.shape, BF16) for w in cast]

    return pl.pallas_call(
        functools.partial(_proj_kernel, nchunks=nchunks, add_pos=add_pos, want_gate=want_gate, n_cast=len(cast)),
        grid=(nchunks,),
        in_specs=in_specs,
        out_specs=out_specs,
        out_shape=out_shape,
        scratch_shapes=[
            pltpu.VMEM((ROWS + 2 * HALO, D_RNN), F32),
            pltpu.VMEM((ROWS, D_RNN), F32),
            pltpu.VMEM((ROWS, D_RNN), F32),
            pltpu.VMEM((ROWS, D_RNN), F32),
            pltpu.VMEM((BATCH, D_RNN), F32),
        ],
        compiler_params=_cparams(),
        name="rec_fwd_lat" if add_pos else "rec_fwd_ctx",
    )(*args)


def _mlp_block(x3, g, sh, sc, gt, w1_ref, w2_ref):
    h = _norm_mod(x3, g, sh, sc).reshape(ROWS, D_MODEL).astype(BF16)
    m = jnp.dot(h, w1_ref[...], preferred_element_type=F32)
    m = jnp.square(jnp.maximum(m, 0.0)).astype(BF16)
    o = jnp.dot(m, w2_ref[...], preferred_element_type=F32)
    return x3 + gt * o.reshape(TT, BATCH, D_MODEL)


def _scan_kernel(*refs, mode):
    it = iter(refs)
    u_ref, h0_ref, lam_ref, wab_ref, bab_ref = next(it), next(it), next(it), next(it), next(it)
    if mode == "out":
        yf_ref, gate_ref, x0_ref, g1_ref, wo_ref = next(it), next(it), next(it), next(it), next(it)
        ng_ref, sh2_ref, sc2_ref, g2_ref, w1_ref, w2_ref = (next(it) for _ in range(6))
        cg_ref, csh_ref, csc_ref, wp_ref, bp_ref = (next(it) for _ in range(5))
        x2_ref, z_ref = next(it), next(it)
    hT_ref = next(it)
    a_scr, b_scr, h_scr = next(it), next(it), next(it)

    i = pl.program_id(0)

    @pl.when(i == 0)
    def _():
        h_scr[...] = h0_ref[...]

    hT_ref[...] = _rglru(None, u_ref, lam_ref, wab_ref, bab_ref, a_scr, b_scr, h_scr, reverse=True)

    if mode == "out":
        y = b_scr[...] + yf_ref[...].astype(F32)
        gy = (gate_ref[...].astype(F32) * y).astype(BF16)
        o = jnp.dot(gy, wo_ref[...], preferred_element_type=F32)
        x1 = x0_ref[...].reshape(TT, BATCH, D_MODEL) + g1_ref[...] * o.reshape(TT, BATCH, D_MODEL)
        x2 = _mlp_block(x1, ng_ref[...], sh2_ref[...], sc2_ref[...], g2_ref[...], w1_ref, w2_ref)
        x2_ref[...] = x2.reshape(ROWS, D_MODEL)
        hc = _norm_mod(x2, cg_ref[...], csh_ref[...], csc_ref[...]).reshape(ROWS, D_MODEL).astype(BF16)
        zz = jnp.dot(hc, wp_ref[...], preferred_element_type=F32) + bp_ref[...]
        z_ref[...] = (zz[:, :D_MODEL] * _sigmoid(zz[:, D_MODEL:])).astype(BF16)


def _rec_scan_rev(u, h0, lam, wab, bab, *, mode, extra=None, mods=None):
    n = u.shape[0]
    nchunks = n // ROWS
    order = lambda i: (nchunks - 1 - i, 0)
    in_specs = [
        pl.BlockSpec((ROWS, D_RNN), order),
        _const_spec((BATCH, D_RNN)),
        _layer_spec((1, D_RNN), 1, 0, 0),
        _layer_spec((N_BD, BD_WIN, 2 * MXU_N), 1, 0, 0, 0),
        _layer_spec((2, D_RNN), 1, 0, 0),
    ]
    args = [u, h0, lam, wab, bab]
    out_specs, out_shape = [], []
    if mode == "out":
        yf, gate, x0, w_out, norm_g, w1, w2, w_pw1, b_pw1 = extra
        in_specs += [
            pl.BlockSpec((ROWS, D_RNN), order),
            pl.BlockSpec((ROWS, D_RNN), order),
            pl.BlockSpec((ROWS, D_MODEL), order),
            _mod_spec(0, False, 2),
            _layer_spec((D_RNN, D_MODEL), 0, 0, 0),
            _layer_spec((1, D_MODEL), 1, 0, 0),
            _mod_spec(0, False, 3),
            _mod_spec(0, False, 4),
            _mod_spec(0, False, 5),
            _layer_spec((D_MODEL, D_FF), 0, 0, 0),
            _layer_spec((D_FF, D_MODEL), 0, 0, 0),
            _layer_spec((1, D_MODEL), 2, 0, 0),
            _mod_spec(1, False, 0),
            _mod_spec(1, False, 1),
            _layer_spec((D_MODEL, 2 * D_MODEL), 0, 0, 0),
            _const_spec((1, 2 * D_MODEL)),
        ]
        args += [yf, gate, x0, mods, w_out, norm_g, mods, mods, mods, w1, w2, norm_g, mods, mods, w_pw1, b_pw1]
        out_specs.append(pl.BlockSpec((ROWS, D_MODEL), order))
        out_shape.append(jax.ShapeDtypeStruct((n, D_MODEL), F32))
        out_specs.append(pl.BlockSpec((ROWS, D_MODEL), order))
        out_shape.append(jax.ShapeDtypeStruct((n, D_MODEL), BF16))
    out_specs.append(pl.BlockSpec((BATCH, D_RNN), lambda i: (0, 0)))
    out_shape.append(jax.ShapeDtypeStruct((BATCH, D_RNN), F32))

    return pl.pallas_call(
        functools.partial(_scan_kernel, mode=mode),
        grid=(nchunks,),
        in_specs=in_specs,
        out_specs=out_specs,
        out_shape=out_shape,
        scratch_shapes=[
            pltpu.VMEM((ROWS, D_RNN), F32),
            pltpu.VMEM((ROWS, D_RNN), F32),
            pltpu.VMEM((BATCH, D_RNN), F32),
        ],
        compiler_params=_cparams(),
        name="rec_rev_" + mode,
    )(*args)


def _conf_out_kernel(zm_ref, zp_ref, zn_ref, x_ref, cw_ref, cb_ref, lg_ref, lb_ref, w_ref, b_ref, g1_ref,
                     ng_ref, sh2_ref, sc2_ref, g2_ref, w1_ref, w2_ref, fg_ref,
                     o_hbm, zext_scr, conv_scr, ybuf, osem, *, nchunks):
    i = pl.program_id(0)
    slot = i % 2

    def out_copies(chunk, slot_):
        return [pltpu.make_async_copy(ybuf.at[slot_, :, b, :], o_hbm.at[b, pl.ds(chunk * TT, TT), :], osem.at[slot_, b])
                for b in range(BATCH)]

    @pl.when(i >= 2)
    def _():
        for cp in out_copies(i - 2, slot):
            cp.wait()

    zext_scr[0:CONF_HALO, :] = zp_ref[...].astype(F32)
    zext_scr[CONF_HALO:CONF_HALO + ROWS, :] = zm_ref[...].astype(F32)
    zext_scr[CONF_HALO + ROWS:, :] = zn_ref[...].astype(F32)

    @pl.when(i == 0)
    def _():
        zext_scr[0:CONF_HALO, :] = jnp.zeros((CONF_HALO, D_MODEL), F32)

    @pl.when(i == nchunks - 1)
    def _():
        zext_scr[CONF_HALO + ROWS:, :] = jnp.zeros((CONF_HALO, D_MODEL), F32)

    _fir(zext_scr, conv_scr, cw_ref, cb_ref, n_taps=CONF_KW, src_off=CONF_HALO - CONF_HALF * BATCH,
         n_out=TT, width=D_MODEL, block=16, unrolled=False)

    z = conv_scr[...]
    mu = jnp.mean(z, axis=-1, keepdims=True)
    zc = z - mu
    var = jnp.mean(zc * zc, axis=-1, keepdims=True)
    zl = zc * lax.rsqrt(var + EPS) * lg_ref[...] + lb_ref[...]
    zs = (zl * _sigmoid(zl)).astype(BF16)
    y = jnp.dot(zs, w_ref[...], preferred_element_type=F32) + b_ref[...]
    x3 = x_ref[...].reshape(TT, BATCH, D_MODEL) + g1_ref[...] * y.reshape(TT, BATCH, D_MODEL)
    x4 = _mlp_block(x3, ng_ref[...], sh2_ref[...], sc2_ref[...], g2_ref[...], w1_ref, w2_ref)
    ybuf[slot] = _rms(x4, fg_ref[...])
    for cp in out_copies(i, slot):
        cp.start()

    @pl.when(i == nchunks - 1)
    def _():
        for cp in out_copies(i - 1, 1 - slot) + out_copies(i, slot):
            cp.wait()


def _conf_out(z, x, mods, layer, norm_g, conv_w, conv_b, ln_g, ln_b, w_pw2, b_pw2, w1, w2, final_g):
    n = x.shape[0]
    nchunks = n // ROWS
    hb = ROWS // CONF_HALO
    return pl.pallas_call(
        functools.partial(_conf_out_kernel, nchunks=nchunks),
        grid=(nchunks,),
        in_specs=[
            pl.BlockSpec((ROWS, D_MODEL), lambda i: (i, 0)),
            pl.BlockSpec((CONF_HALO, D_MODEL), lambda i: (jnp.maximum(i * hb - 1, 0), 0)),
            pl.BlockSpec((CONF_HALO, D_MODEL), lambda i: (jnp.minimum((i + 1) * hb, n // CONF_HALO - 1), 0)),
            pl.BlockSpec((ROWS, D_MODEL), lambda i: (i, 0)),
            _layer_spec((CONF_KW, D_MODEL), 0, 0, 0),
            _const_spec((1, D_MODEL)),
            _const_spec((1, D_MODEL)),
            _const_spec((1, D_MODEL)),
            _layer_spec((D_MODEL, D_MODEL), 0, 0, 0),
            _const_spec((1, D_MODEL)),
            _mod_spec(layer, False, 2),
            _layer_spec((1, D_MODEL), 2 * layer + 1, 0, 0),
            _mod_spec(layer, False, 3),
            _mod_spec(layer, False, 4),
            _mod_spec(layer, False, 5),
            _layer_spec((D_MODEL, D_FF), layer, 0, 0),
            _layer_spec((D_FF, D_MODEL), layer, 0, 0),
            _const_spec((1, D_MODEL)),
        ],
        out_specs=pl.BlockSpec(memory_space=pl.ANY),
        out_shape=jax.ShapeDtypeStruct((BATCH, n // BATCH, D_MODEL), F32),
        scratch_shapes=[
            pltpu.VMEM((ROWS + 2 * CONF_HALO, D_MODEL), F32),
            pltpu.VMEM((ROWS, D_MODEL), F32),
            pltpu.VMEM((2, TT, BATCH, D_MODEL), F32),
            pltpu.SemaphoreType.DMA((2, BATCH)),
        ],
        compiler_params=_cparams(),
        name="conf_out",
    )(z, z, z, x, conv_w, conv_b, ln_g, ln_b, w_pw2, b_pw2, mods, norm_g, mods, mods, mods, w1, w2, final_g)


def _pos_tables(rows, d):
    q = d // 4
    omega = 1.0 / (POS_BASE ** (jnp.arange(q, dtype=F32) / q))
    er = jnp.arange(rows, dtype=jnp.int32).astype(F32)[:, None] * omega[None, :]
    ec = jnp.arange(GRID_W, dtype=jnp.int32).astype(F32)[:, None] * omega[None, :]
    rowtab = jnp.concatenate([jnp.sin(er), jnp.cos(er)], axis=-1)
    coltab = jnp.concatenate([jnp.sin(ec), jnp.cos(ec)], axis=-1)
    return rowtab.reshape(rows, 1, d // 2), coltab.reshape(GRID_W, 1, d // 2)


def _block_diag_windows(w_a, w_x):
    w = jnp.stack([w_a, w_x], axis=1).reshape(2, 2, D_RNN, RNN_BLOCK)
    tiled_eye = np.tile(np.eye(RNN_BLOCK, dtype=np.float32), (1, N_RNN_BLOCKS))
    row_blk = np.arange(D_RNN)[:, None] // RNN_BLOCK
    col_blk = np.arange(D_RNN)[None, :] // RNN_BLOCK
    w_win, sel, msk = [], [], []
    for j, k0 in enumerate(BD_K0):
        cols = slice(j * MXU_N, (j + 1) * MXU_N)
        w_win.append(w[:, :, k0:k0 + BD_WIN])
        sel.append(tiled_eye[:, cols])
        msk.append(0.5 * (row_blk[k0:k0 + BD_WIN] == col_blk[:, cols]))
    w_win = jnp.stack(w_win, axis=2)
    sel = jnp.asarray(np.stack(sel))
    msk = jnp.asarray(np.stack(msk), dtype=F32)
    full = jnp.einsum("dgjrk,jkc->djrgc", w_win, sel) * msk[None, :, :, None, :]
    return full.reshape(2, N_BD, BD_WIN, 2 * MXU_N).astype(BF16)


def kernel(x, c, ctx, c_ctx, w_ada, b_ada, norm_g, rec_w_in, rec_conv_w, rec_conv_b, rec_lambda, rec_w_a, rec_b_a,
           rec_w_x, rec_b_x, rec_w_out, conf_w_pw1, conf_b_pw1, conf_conv_w, conf_conv_b, conf_ln_g, conf_ln_b,
           conf_w_pw2, conf_b_pw2, mlp_w_in, mlp_w_out, final_g):
    assert x.shape == (BATCH, SEQ, D_MODEL) and ctx.shape == (BATCH, CTX_LEN, D_MODEL)
    assert w_ada.shape[0] == 2, "one recurrent layer followed by one conformer layer"

    cc = jnp.concatenate([c, jnp.broadcast_to(c_ctx[None, :], (BATCH, D_MODEL))], axis=0)
    mods = _ada_table(cc, w_ada, b_ada)
    norm_g = norm_g.reshape(4, 1, D_MODEL)

    w_in = rec_w_in.astype(BF16)
    conv_b = rec_conv_b.reshape(1, 1, D_RNN)
    wab = _block_diag_windows(rec_w_a[0], rec_w_x[0])
    bab = 0.5 * jnp.stack([rec_b_a[0].reshape(2, D_RNN), rec_b_x[0].reshape(2, D_RNN)], axis=1)
    lam = rec_lambda[0].reshape(2, 1, D_RNN)
    zeros = jnp.zeros((BATCH, D_RNN), F32)
    rec = (lam, wab, bab)

    u_ctx, h0_f = _rec_project(ctx, None, mods, True, norm_g, w_in, rec_conv_w, conv_b, zeros, *rec, False)
    (h0_r,) = _rec_scan_rev(u_ctx, zeros, *rec, mode="state")

    pos_tabs = _pos_tables(SEQ // GRID_W, D_MODEL)
    x0, u_lat, gate, y_f, _, mlp_w1, mlp_w2, w_pw1, w_pw2 = _rec_project(
        x, pos_tabs, mods, False, norm_g, w_in, rec_conv_w, conv_b, h0_f, *rec, True,
        cast=(mlp_w_in, mlp_w_out, conf_w_pw1, conf_w_pw2))
    x2, z, _ = _rec_scan_rev(u_lat, h0_r, *rec, mode="out", mods=mods,
                             extra=(y_f, gate, x0, rec_w_out.astype(BF16), norm_g, mlp_w1, mlp_w2,
                                    w_pw1, conf_b_pw1.reshape(1, 2 * D_MODEL)))

    return _conf_out(z, x2, mods, 1, norm_g, conf_conv_w, conf_conv_b.reshape(1, D_MODEL),
                     conf_ln_g.reshape(1, D_MODEL), conf_ln_b.reshape(1, D_MODEL),
                     w_pw2, conf_b_pw2.reshape(1, D_MODEL), mlp_w1, mlp_w2, final_g.reshape(1, D_MODEL))
```

```python
import functools

import jax
import jax.numpy as jnp
import numpy as np
from jax import lax
from jax.experimental import pallas as pl
from jax.experimental.pallas import tpu as pltpu

F32 = jnp.float32
BF16 = jnp.bfloat16

D_MODEL = 1024
BATCH = 8
SEQ = 2048
CTX_LEN = 256
GRID_W = 64
D_RNN = 1280
N_RNN_BLOCKS = 16
RNN_BLOCK = D_RNN // N_RNN_BLOCKS
REC_CONV_W = 4
RG_C = 8.0
CONF_KW = 31
CONF_HALF = CONF_KW // 2
D_FF = 4 * D_MODEL
N_MOD = 6
EPS = 1e-6
POS_BASE = 10000.0

VMEM_LIMIT_BYTES = 56 * 1024 * 1024
LANES = 128
MXU_N = 256
BD_WIN = 512
BD_K0 = (0, 128, 384, 640, 768)
N_BD = D_RNN // MXU_N

TT = 64
ROWS = TT * BATCH
HALO = 16
HALO_T = HALO // BATCH
XHALO_T = 8
CONF_HALO = 128


def _cparams():
    return pltpu.CompilerParams(dimension_semantics=("arbitrary",), vmem_limit_bytes=VMEM_LIMIT_BYTES)


def _const_spec(shape):
    return pl.BlockSpec(shape, lambda *_: (0,) * len(shape), pipeline_mode=pl.Buffered(1))


def _layer_spec(shape, *idx):
    return pl.BlockSpec((None,) + tuple(shape), lambda *_: tuple(idx), pipeline_mode=pl.Buffered(1))


def _mod_spec(layer, ctx, k):
    return _layer_spec((BATCH, D_MODEL), layer, int(ctx), k)


def _sigmoid(x):
    return 0.5 * jnp.tanh(0.5 * x) + 0.5


def _rms(x3, g):
    ms = jnp.mean(x3 * x3, axis=-1, keepdims=True)
    return x3 * lax.rsqrt(ms + EPS) * g


def _norm_mod(x3, g, sh, sc):
    return _rms(x3, g) * (1.0 + sc) + sh


def _ada_kernel(c_ref, w_ref, b_ref, o_ref):
    c = c_ref[...]
    s = (c * jax.nn.sigmoid(c)).astype(BF16)
    o_ref[...] = jnp.dot(s, w_ref[...].astype(BF16), preferred_element_type=F32) + b_ref[...]


def _ada_table(cc, w_ada, b_ada):
    depth = w_ada.shape[0]
    tn = 1536
    return pl.pallas_call(
        _ada_kernel,
        grid=(depth, N_MOD * D_MODEL // tn),
        in_specs=[
            pl.BlockSpec((2 * BATCH, D_MODEL), lambda l, j: (0, 0)),
            pl.BlockSpec((None, D_MODEL, tn), lambda l, j: (l, 0, j)),
            pl.BlockSpec((None, 1, tn), lambda l, j: (l, 0, j)),
        ],
        out_specs=pl.BlockSpec((None, 2 * BATCH, tn), lambda l, j: (l, 0, j)),
        out_shape=jax.ShapeDtypeStruct((depth, 2 * BATCH, N_MOD * D_MODEL), F32),
        compiler_params=pltpu.CompilerParams(
            dimension_semantics=("arbitrary", "arbitrary"), vmem_limit_bytes=VMEM_LIMIT_BYTES),
        name="ada_table",
    )(cc, w_ada, b_ada.reshape(depth, 1, N_MOD * D_MODEL))


def _fir(src_ref, dst_ref, w_ref, b_ref, *, n_taps, src_off, n_out, width, block, unrolled):
    for c in range(width // LANES):
        lanes = slice(c * LANES, (c + 1) * LANES)
        wk = [jnp.broadcast_to(w_ref[k:k + 1, lanes], (BATCH, LANES)) for k in range(n_taps)]
        bias = jnp.broadcast_to(b_ref[:, lanes], (BATCH, LANES))

        def one_block(base):
            acc = [bias] * block
            for q in range(block + n_taps - 1):
                z = src_ref[pl.ds(base + src_off + q * BATCH, BATCH), lanes]
                for j in range(max(0, q - n_taps + 1), min(block, q + 1)):
                    acc[j] = acc[j] + wk[q - j] * z
            for j in range(block):
                dst_ref[pl.ds(base + j * BATCH, BATCH), lanes] = acc[j]

        if unrolled:
            for blk in range(n_out // block):
                one_block(blk * block * BATCH)
        else:
            def body(blk, carry):
                one_block(pl.multiple_of(blk * (block * BATCH), block * BATCH))
                return carry

            lax.fori_loop(0, n_out // block, body, 0, unroll=2)


PACK = 16
FIR_GROUP = 4


def _fir_packed(ze_scr, zo_scr, dst_ref, w_ref, b_ref, *, n_taps, n_out, width, block):
    for c in range(width // LANES):
        lanes = slice(c * LANES, (c + 1) * LANES)
        wk = [jnp.broadcast_to(w_ref[k:k + 1, lanes].astype(BF16), (PACK, LANES)) for k in range(n_taps)]
        bias = jnp.broadcast_to(b_ref[:, lanes], (PACK, LANES))

        def body(blk, carry):
            base = pl.multiple_of(blk * (block * PACK), block * PACK)
            acc = [bias] * block
            part = [None] * block
            cnt = [0] * block
            for tau in range(1, 2 * block + n_taps):
                src, j = (ze_scr, tau // 2) if tau % 2 == 0 else (zo_scr, (tau - 1) // 2)
                zt = src[pl.ds(base + j * PACK, PACK), lanes]
                for p in range(block):
                    k = tau - 2 * p - 1
                    if 0 <= k < n_taps:
                        prod = wk[k] * zt
                        part[p] = prod if part[p] is None else part[p] + prod
                        cnt[p] += 1
                        if cnt[p] % FIR_GROUP == 0 or cnt[p] == n_taps:
                            acc[p] = acc[p] + part[p].astype(F32)
                            part[p] = None
            for p in range(block):
                dst_ref[pl.ds(base + p * PACK, PACK), lanes] = acc[p]
            return carry

        lax.fori_loop(0, (n_out * BATCH // PACK) // block, body, 0, unroll=2)


SCAN_RB = 128
LOG2_E = 1.4426950408889634


def _rglru(u_ref, ub_ref, lam_ref, wab_ref, bab_ref, a_scr, b_scr, h_scr, reverse):
    lam = lam_ref[...]
    softplus = jnp.maximum(-lam, 0.0) + jnp.log1p(jnp.exp(-jnp.abs(lam)))
    k = (-0.5 * RG_C * LOG2_E) * softplus
    n_rb = ROWS // SCAN_RB
    h = h_scr[...]
    for rb in (reversed(range(n_rb)) if reverse else range(n_rb)):
        rows = slice(rb * SCAN_RB, (rb + 1) * SCAN_RB)
        for j in range(N_BD):
            cols = slice(j * MXU_N, (j + 1) * MXU_N)
            pre = jnp.dot(ub_ref[rows, BD_K0[j]:BD_K0[j] + BD_WIN], wab_ref[j],
                          preferred_element_type=F32)
            u = u_ref[rows, cols] if u_ref is not None else ub_ref[rows, cols].astype(F32)
            t_r = jnp.tanh(pre[:, :MXU_N] + bab_ref[0:1, cols])
            t_i = jnp.tanh(pre[:, MXU_N:] + bab_ref[1:2, cols])
            a = jnp.exp2(k[:, cols] * t_r + k[:, cols])
            w = 1.0 - a * a
            sq = jnp.where(w > 0.0, w * lax.rsqrt(w), 0.0)
            a_scr[rows, cols] = a
            b_scr[rows, cols] = (0.5 * sq) * ((t_i + 1.0) * u)
        steps = range(rb * SCAN_RB // BATCH, (rb + 1) * SCAN_RB // BATCH)
        for t in (reversed(steps) if reverse else steps):
            r8 = slice(t * BATCH, (t + 1) * BATCH)
            h = a_scr[r8, :] * h + b_scr[r8, :]
            b_scr[r8, :] = h
    h_scr[...] = h
    return h


def _proj_kernel(*refs, nchunks, add_pos, want_gate, n_cast):
    it = iter(refs)
    xm_ref, xp_ref, xn_ref = next(it), next(it), next(it)
    if add_pos:
        rowtab_ref, coltab_ref = next(it), next(it)
    g_ref, sh_ref, sc_ref = next(it), next(it), next(it)
    wr_ref, cw_ref, cb_ref = next(it), next(it), next(it)
    h0_ref, lam_ref, wab_ref, bab_ref = next(it), next(it), next(it), next(it)
    if want_gate:
        wg_ref = next(it)
    cast_in = [next(it) for _ in range(n_cast)]
    if add_pos:
        x0_ref = next(it)
    u_ref = next(it)
    if want_gate:
        gate_ref = next(it)
        yf_ref = next(it)
    hT_ref = next(it)
    cast_out = [next(it) for _ in range(n_cast)]
    zr_scr, u_scr, a_scr, b_scr, h_scr = next(it), next(it), next(it), next(it), next(it)

    for src, dst in zip(cast_in, cast_out):
        dst[...] = src[...].astype(BF16)

    i = pl.program_id(0)

    @pl.when(i == 0)
    def _():
        h_scr[...] = h0_ref[...]

    xm = jnp.swapaxes(xm_ref[...], 0, 1)
    xp = jnp.swapaxes(xp_ref[...], 0, 1)[XHALO_T - HALO_T:]
    xn = jnp.swapaxes(xn_ref[...], 0, 1)[:HALO_T]
    x3 = jnp.concatenate([xp, xm, xn], axis=0)
    if add_pos:
        half = D_MODEL // 2
        r_prev = rowtab_ref[pl.ds(jnp.maximum(i - 1, 0), 1)]
        r_here = rowtab_ref[pl.ds(i, 1)]
        r_next = rowtab_ref[pl.ds(jnp.minimum(i + 1, nchunks - 1), 1)]
        prow = jnp.concatenate([jnp.broadcast_to(r_prev, (HALO_T, 1, half)),
                                jnp.broadcast_to(r_here, (TT, 1, half)),
                                jnp.broadcast_to(r_next, (HALO_T, 1, half))], axis=0)
        pcol = jnp.concatenate([coltab_ref[GRID_W - HALO_T:], coltab_ref[...], coltab_ref[:HALO_T]], axis=0)
        x3 = x3 + jnp.concatenate([prow, pcol], axis=-1)
        x0_ref[...] = x3[HALO_T:HALO_T + TT].reshape(ROWS, D_MODEL)
    h = _norm_mod(x3, g_ref[...], sh_ref[...], sc_ref[...])
    keep_prev = jnp.where(i == 0, 0.0, 1.0)
    keep_next = jnp.where(i == nchunks - 1, 0.0, 1.0)
    h = jnp.concatenate([h[:HALO_T] * keep_prev, h[HALO_T:HALO_T + TT], h[HALO_T + TT:] * keep_next], axis=0)
    h = h.reshape(ROWS + 2 * HALO, D_MODEL).astype(BF16)

    zr_scr[...] = jnp.dot(h, wr_ref[...], preferred_element_type=F32)
    if want_gate:
        zg = jnp.dot(h[HALO:HALO + ROWS], wg_ref[...], preferred_element_type=F32)
        gate_ref[...] = jax.nn.gelu(zg).astype(BF16)

    _fir(zr_scr, u_scr, cw_ref, cb_ref, n_taps=REC_CONV_W, src_off=HALO - BATCH, n_out=TT, width=D_RNN, block=16,
         unrolled=True)
    u_ref[...] = u_scr[...].astype(BF16)

    hT_ref[...] = _rglru(u_scr, u_ref, lam_ref, wab_ref, bab_ref, a_scr, b_scr, h_scr, reverse=False)
    if want_gate:
        yf_ref[...] = b_scr[...].astype(BF16)


def _rec_project(x, pos_tabs, mods, ctx, norm_g, w_in, conv_w, conv_b, h0, lam, wab, bab, want_gate, cast=()):
    t_len = x.shape[1]
    n = t_len * BATCH
    nchunks = t_len // TT
    add_pos = pos_tabs is not None
    hb = TT // XHALO_T
    nhb = t_len // XHALO_T

    in_specs = [
        pl.BlockSpec((BATCH, TT, D_MODEL), lambda i: (0, i, 0)),
        pl.BlockSpec((BATCH, XHALO_T, D_MODEL), lambda i: (0, jnp.maximum(i * hb - 1, 0), 0)),
        pl.BlockSpec((BATCH, XHALO_T, D_MODEL), lambda i: (0, jnp.minimum((i + 1) * hb, nhb - 1), 0)),
    ]
    args = [x, x, x]
    if add_pos:
        assert TT == GRID_W and pos_tabs[0].shape[0] == nchunks
        in_specs += [_const_spec(pos_tabs[0].shape), _const_spec(pos_tabs[1].shape)]
        args += list(pos_tabs)
    in_specs += [
        _layer_spec((1, D_MODEL), 0, 0, 0),
        _mod_spec(0, ctx, 0),
        _mod_spec(0, ctx, 1),
        _layer_spec((D_MODEL, D_RNN), 0, 0, 1),
        _layer_spec((REC_CONV_W, D_RNN), 0, 0, 0),
        _layer_spec((1, D_RNN), 0, 0, 0),
    ]
    args += [norm_g, mods, mods, w_in, conv_w, conv_b]
    in_specs += [
        _const_spec((BATCH, D_RNN)),
        _layer_spec((1, D_RNN), 0, 0, 0),
        _layer_spec((N_BD, BD_WIN, 2 * MXU_N), 0, 0, 0, 0),
        _layer_spec((2, D_RNN), 0, 0, 0),
    ]
    args += [h0, lam, wab, bab]
    if want_gate:
        in_specs.append(_layer_spec((D_MODEL, D_RNN), 0, 0, 0))
        args.append(w_in)
    cast_specs = []
    for w in cast:
        slab = (w.shape[0], w.shape[1] // nchunks, w.shape[2])
        assert w.shape[1] % nchunks == 0 and slab[1] % 16 == 0, w.shape
        cast_specs.append(pl.BlockSpec(slab, lambda i: (0, i, 0)))
    in_specs += cast_specs
    args += list(cast)

    out_specs, out_shape = [], []
    if add_pos:
        out_specs.append(pl.BlockSpec((ROWS, D_MODEL), lambda i: (i, 0)))
        out_shape.append(jax.ShapeDtypeStruct((n, D_MODEL), F32))
    out_specs.append(pl.BlockSpec((ROWS, D_RNN), lambda i: (i, 0)))
    out_shape.append(jax.ShapeDtypeStruct((n, D_RNN), BF16))
    if want_gate:
        out_specs.append(pl.BlockSpec((ROWS, D_RNN), lambda i: (i, 0)))
        out_shape.append(jax.ShapeDtypeStruct((n, D_RNN), BF16))
        out_specs.append(pl.BlockSpec((ROWS, D_RNN), lambda i: (i, 0)))
        out_shape.append(jax.ShapeDtypeStruct((n, D_RNN), BF16))
    out_specs.append(pl.BlockSpec((BATCH, D_RNN), lambda i: (0, 0)))
    out_shape.append(jax.ShapeDtypeStruct((BATCH, D_RNN), F32))
    out_specs += cast_specs
    out_shape += [jax.ShapeDtypeStruct(w.shape, BF16) for w in cast]

    return pl.pallas_call(
        functools.partial(_proj_kernel, nchunks=nchunks, add_pos=add_pos, want_gate=want_gate, n_cast=len(cast)),
        grid=(nchunks,),
        in_specs=in_specs,
        out_specs=out_specs,
        out_shape=out_shape,
        scratch_shapes=[
            pltpu.VMEM((ROWS + 2 * HALO, D_RNN), F32),
            pltpu.VMEM((ROWS, D_RNN), F32),
            pltpu.VMEM((ROWS, D_RNN), F32),
            pltpu.VMEM((ROWS, D_RNN), F32),
            pltpu.VMEM((BATCH, D_RNN), F32),
        ],
        compiler_params=_cparams(),
        name="rec_fwd_lat" if add_pos else "rec_fwd_ctx",
    )(*args)


def _mlp_block(x3, g, sh, sc, gt, w1_ref, w2_ref):
    h = _norm_mod(x3, g, sh, sc).reshape(ROWS, D_MODEL).astype(BF16)
    m = jnp.dot(h, w1_ref[...], preferred_element_type=F32)
    m = jnp.square(jnp.maximum(m, 0.0)).astype(BF16)
    o = jnp.dot(m, w2_ref[...], preferred_element_type=F32)
    return x3 + gt * o.reshape(TT, BATCH, D_MODEL)


def _scan_kernel(*refs, mode):
    it = iter(refs)
    u_ref, h0_ref, lam_ref, wab_ref, bab_ref = next(it), next(it), next(it), next(it), next(it)
    if mode == "out":
        yf_ref, gate_ref, x0_ref, g1_ref, wo_ref = next(it), next(it), next(it), next(it), next(it)
        ng_ref, sh2_ref, sc2_ref, g2_ref, w1_ref, w2_ref = (next(it) for _ in range(6))
        cg_ref, csh_ref, csc_ref, wp_ref, bp_ref = (next(it) for _ in range(5))
        x2_ref, z_ref = next(it), next(it)
    hT_ref = next(it)
    a_scr, b_scr, h_scr = next(it), next(it), next(it)

    i = pl.program_id(0)

    @pl.when(i == 0)
    def _():
        h_scr[...] = h0_ref[...]

    hT_ref[...] = _rglru(None, u_ref, lam_ref, wab_ref, bab_ref, a_scr, b_scr, h_scr, reverse=True)

    if mode == "out":
        y = b_scr[...] + yf_ref[...].astype(F32)
        gy = (gate_ref[...].astype(F32) * y).astype(BF16)
        o = jnp.dot(gy, wo_ref[...], preferred_element_type=F32)
        x1 = x0_ref[...].reshape(TT, BATCH, D_MODEL) + g1_ref[...] * o.reshape(TT, BATCH, D_MODEL)
        x2 = _mlp_block(x1, ng_ref[...], sh2_ref[...], sc2_ref[...], g2_ref[...], w1_ref, w2_ref)
        x2_ref[...] = x2.reshape(ROWS, D_MODEL)
        hc = _norm_mod(x2, cg_ref[...], csh_ref[...], csc_ref[...]).reshape(ROWS, D_MODEL).astype(BF16)
        zz = jnp.dot(hc, wp_ref[...], preferred_element_type=F32) + bp_ref[...]
        z_ref[...] = (zz[:, :D_MODEL] * _sigmoid(zz[:, D_MODEL:])).astype(BF16)


def _rec_scan_rev(u, h0, lam, wab, bab, *, mode, extra=None, mods=None):
    n = u.shape[0]
    nchunks = n // ROWS
    order = lambda i: (nchunks - 1 - i, 0)
    in_specs = [
        pl.BlockSpec((ROWS, D_RNN), order),
        _const_spec((BATCH, D_RNN)),
        _layer_spec((1, D_RNN), 1, 0, 0),
        _layer_spec((N_BD, BD_WIN, 2 * MXU_N), 1, 0, 0, 0),
        _layer_spec((2, D_RNN), 1, 0, 0),
    ]
    args = [u, h0, lam, wab, bab]
    out_specs, out_shape = [], []
    if mode == "out":
        yf, gate, x0, w_out, norm_g, w1, w2, w_pw1, b_pw1 = extra
        in_specs += [
            pl.BlockSpec((ROWS, D_RNN), order),
            pl.BlockSpec((ROWS, D_RNN), order),
            pl.BlockSpec((ROWS, D_MODEL), order),
            _mod_spec(0, False, 2),
            _layer_spec((D_RNN, D_MODEL), 0, 0, 0),
            _layer_spec((1, D_MODEL), 1, 0, 0),
            _mod_spec(0, False, 3),
            _mod_spec(0, False, 4),
            _mod_spec(0, False, 5),
            _layer_spec((D_MODEL, D_FF), 0, 0, 0),
            _layer_spec((D_FF, D_MODEL), 0, 0, 0),
            _layer_spec((1, D_MODEL), 2, 0, 0),
            _mod_spec(1, False, 0),
            _mod_spec(1, False, 1),
            _layer_spec((D_MODEL, 2 * D_MODEL), 0, 0, 0),
            _const_spec((1, 2 * D_MODEL)),
        ]
        args += [yf, gate, x0, mods, w_out, norm_g, mods, mods, mods, w1, w2, norm_g, mods, mods, w_pw1, b_pw1]
        out_specs.append(pl.BlockSpec((ROWS, D_MODEL), order))
        out_shape.append(jax.ShapeDtypeStruct((n, D_MODEL), F32))
        out_specs.append(pl.BlockSpec((ROWS, D_MODEL), order))
        out_shape.append(jax.ShapeDtypeStruct((n, D_MODEL), BF16))
    out_specs.append(pl.BlockSpec((BATCH, D_RNN), lambda i: (0, 0)))
    out_shape.append(jax.ShapeDtypeStruct((BATCH, D_RNN), F32))

    return pl.pallas_call(
        functools.partial(_scan_kernel, mode=mode),
        grid=(nchunks,),
        in_specs=in_specs,
        out_specs=out_specs,
        out_shape=out_shape,
        scratch_shapes=[
            pltpu.VMEM((ROWS, D_RNN), F32),
            pltpu.VMEM((ROWS, D_RNN), F32),
            pltpu.VMEM((BATCH, D_RNN), F32),
        ],
        compiler_params=_cparams(),
        name="rec_rev_" + mode,
    )(*args)


def _conf_out_kernel(zm_ref, zp_ref, zn_ref, x_ref, cw_ref, cb_ref, lg_ref, lb_ref, w_ref, b_ref, g1_ref,
                     ng_ref, sh2_ref, sc2_ref, g2_ref, w1_ref, w2_ref, fg_ref,
                     o_hbm, zext_scr, conv_scr, ze_scr, zo_scr, ybuf, osem, *, nchunks):
    i = pl.program_id(0)
    slot = i % 2

    def out_copies(chunk, slot_):
        return [pltpu.make_async_copy(ybuf.at[slot_, :, b, :], o_hbm.at[b, pl.ds(chunk * TT, TT), :], osem.at[slot_, b])
                for b in range(BATCH)]

    @pl.when(i >= 2)
    def _():
        for cp in out_copies(i - 2, slot):
            cp.wait()

    zext_scr[0:CONF_HALO, :] = zp_ref[...].astype(F32)
    zext_scr[CONF_HALO:CONF_HALO + ROWS, :] = zm_ref[...].astype(F32)
    zext_scr[CONF_HALO + ROWS:, :] = zn_ref[...].astype(F32)

    @pl.when(i == 0)
    def _():
        zext_scr[0:CONF_HALO, :] = jnp.zeros((CONF_HALO, D_MODEL), F32)

    @pl.when(i == nchunks - 1)
    def _():
        zext_scr[CONF_HALO + ROWS:, :] = jnp.zeros((CONF_HALO, D_MODEL), F32)

    n_even = ROWS + 2 * CONF_HALO
    ze_scr[...] = zext_scr[0:n_even, :].astype(BF16)
    zo_scr[...] = zext_scr[BATCH:BATCH + n_even - PACK, :].astype(BF16)
    _fir_packed(ze_scr, zo_scr, conv_scr, cw_ref, cb_ref, n_taps=CONF_KW, n_out=TT, width=D_MODEL, block=8)

    z = conv_scr[...]
    mu = jnp.mean(z, axis=-1, keepdims=True)
    zc = z - mu
    var = jnp.mean(zc * zc, axis=-1, keepdims=True)
    zl = zc * lax.rsqrt(var + EPS) * lg_ref[...] + lb_ref[...]
    zs = (zl * _sigmoid(zl)).astype(BF16)
    y = jnp.dot(zs, w_ref[...], preferred_element_type=F32) + b_ref[...]
    x3 = x_ref[...].reshape(TT, BATCH, D_MODEL) + g1_ref[...] * y.reshape(TT, BATCH, D_MODEL)
    x4 = _mlp_block(x3, ng_ref[...], sh2_ref[...], sc2_ref[...], g2_ref[...], w1_ref, w2_ref)
    ybuf[slot] = _rms(x4, fg_ref[...])
    for cp in out_copies(i, slot):
        cp.start()

    @pl.when(i == nchunks - 1)
    def _():
        for cp in out_copies(i - 1, 1 - slot) + out_copies(i, slot):
            cp.wait()


def _conf_out(z, x, mods, layer, norm_g, conv_w, conv_b, ln_g, ln_b, w_pw2, b_pw2, w1, w2, final_g):
    n = x.shape[0]
    nchunks = n // ROWS
    hb = ROWS // CONF_HALO
    return pl.pallas_call(
        functools.partial(_conf_out_kernel, nchunks=nchunks),
        grid=(nchunks,),
        in_specs=[
            pl.BlockSpec((ROWS, D_MODEL), lambda i: (i, 0)),
            pl.BlockSpec((CONF_HALO, D_MODEL), lambda i: (jnp.maximum(i * hb - 1, 0), 0)),
            pl.BlockSpec((CONF_HALO, D_MODEL), lambda i: (jnp.minimum((i + 1) * hb, n // CONF_HALO - 1), 0)),
            pl.BlockSpec((ROWS, D_MODEL), lambda i: (i, 0)),
            _layer_spec((CONF_KW, D_MODEL), 0, 0, 0),
            _const_spec((1, D_MODEL)),
            _const_spec((1, D_MODEL)),
            _const_spec((1, D_MODEL)),
            _layer_spec((D_MODEL, D_MODEL), 0, 0, 0),
            _const_spec((1, D_MODEL)),
            _mod_spec(layer, False, 2),
            _layer_spec((1, D_MODEL), 2 * layer + 1, 0, 0),
            _mod_spec(layer, False, 3),
            _mod_spec(layer, False, 4),
            _mod_spec(layer, False, 5),
            _layer_spec((D_MODEL, D_FF), layer, 0, 0),
            _layer_spec((D_FF, D_MODEL), layer, 0, 0),
            _const_spec((1, D_MODEL)),
        ],
        out_specs=pl.BlockSpec(memory_space=pl.ANY),
        out_shape=jax.ShapeDtypeStruct((BATCH, n // BATCH, D_MODEL), F32),
        scratch_shapes=[
            pltpu.VMEM((ROWS + 2 * CONF_HALO, D_MODEL), F32),
            pltpu.VMEM((ROWS, D_MODEL), F32),
            pltpu.VMEM((ROWS + 2 * CONF_HALO, D_MODEL), BF16),
            pltpu.VMEM((ROWS + 2 * CONF_HALO - PACK, D_MODEL), BF16),
            pltpu.VMEM((2, TT, BATCH, D_MODEL), F32),
            pltpu.SemaphoreType.DMA((2, BATCH)),
        ],
        compiler_params=_cparams(),
        name="conf_out",
    )(z, z, z, x, conv_w, conv_b, ln_g, ln_b, w_pw2, b_pw2, mods, norm_g, mods, mods, mods, w1, w2, final_g)


def _pos_tables(rows, d):
    q = d // 4
    omega = 1.0 / (POS_BASE ** (jnp.arange(q, dtype=F32) / q))
    er = jnp.arange(rows, dtype=jnp.int32).astype(F32)[:, None] * omega[None, :]
    ec = jnp.arange(GRID_W, dtype=jnp.int32).astype(F32)[:, None] * omega[None, :]
    rowtab = jnp.concatenate([jnp.sin(er), jnp.cos(er)], axis=-1)
    coltab = jnp.concatenate([jnp.sin(ec), jnp.cos(ec)], axis=-1)
    return rowtab.reshape(rows, 1, d // 2), coltab.reshape(GRID_W, 1, d // 2)


def _block_diag_windows(w_a, w_x):
    w = jnp.stack([w_a, w_x], axis=1).reshape(2, 2, D_RNN, RNN_BLOCK)
    tiled_eye = np.tile(np.eye(RNN_BLOCK, dtype=np.float32), (1, N_RNN_BLOCKS))
    row_blk = np.arange(D_RNN)[:, None] // RNN_BLOCK
    col_blk = np.arange(D_RNN)[None, :] // RNN_BLOCK
    w_win, sel, msk = [], [], []
    for j, k0 in enumerate(BD_K0):
        cols = slice(j * MXU_N, (j + 1) * MXU_N)
        w_win.append(w[:, :, k0:k0 + BD_WIN])
        sel.append(tiled_eye[:, cols])
        msk.append(0.5 * (row_blk[k0:k0 + BD_WIN] == col_blk[:, cols]))
    w_win = jnp.stack(w_win, axis=2)
    sel = jnp.asarray(np.stack(sel))
    msk = jnp.asarray(np.stack(msk), dtype=F32)
    full = jnp.einsum("dgjrk,jkc->djrgc", w_win, sel) * msk[None, :, :, None, :]
    return full.reshape(2, N_BD, BD_WIN, 2 * MXU_N).astype(BF16)


def kernel(x, c, ctx, c_ctx, w_ada, b_ada, norm_g, rec_w_in, rec_conv_w, rec_conv_b, rec_lambda, rec_w_a, rec_b_a,
           rec_w_x, rec_b_x, rec_w_out, conf_w_pw1, conf_b_pw1, conf_conv_w, conf_conv_b, conf_ln_g, conf_ln_b,
           conf_w_pw2, conf_b_pw2, mlp_w_in, mlp_w_out, final_g):
    assert x.shape == (BATCH, SEQ, D_MODEL) and ctx.shape == (BATCH, CTX_LEN, D_MODEL)
    assert w_ada.shape[0] == 2, "one recurrent layer followed by one conformer layer"

    cc = jnp.concatenate([c, jnp.broadcast_to(c_ctx[None, :], (BATCH, D_MODEL))], axis=0)
    mods = _ada_table(cc, w_ada, b_ada)
    norm_g = norm_g.reshape(4, 1, D_MODEL)

    w_in = rec_w_in.astype(BF16)
    conv_b = rec_conv_b.reshape(1, 1, D_RNN)
    wab = _block_diag_windows(rec_w_a[0], rec_w_x[0])
    bab = 0.5 * jnp.stack([rec_b_a[0].reshape(2, D_RNN), rec_b_x[0].reshape(2, D_RNN)], axis=1)
    lam = rec_lambda[0].reshape(2, 1, D_RNN)
    zeros = jnp.zeros((BATCH, D_RNN), F32)
    rec = (lam, wab, bab)

    u_ctx, h0_f = _rec_project(ctx, None, mods, True, norm_g, w_in, rec_conv_w, conv_b, zeros, *rec, False)
    (h0_r,) = _rec_scan_rev(u_ctx, zeros, *rec, mode="state")

    pos_tabs = _pos_tables(SEQ // GRID_W, D_MODEL)
    x0, u_lat, gate, y_f, _, mlp_w1, mlp_w2, w_pw1, w_pw2 = _rec_project(
        x, pos_tabs, mods, False, norm_g, w_in, rec_conv_w, conv_b, h0_f, *rec, True,
        cast=(mlp_w_in, mlp_w_out, conf_w_pw1, conf_w_pw2))
    x2, z, _ = _rec_scan_rev(u_lat, h0_r, *rec, mode="out", mods=mods,
                             extra=(y_f, gate, x0, rec_w_out.astype(BF16), norm_g, mlp_w1, mlp_w2,
                                    w_pw1, conf_b_pw1.reshape(1, 2 * D_MODEL)))

    return _conf_out(z, x2, mods, 1, norm_g, conf_conv_w, conf_conv_b.reshape(1, D_MODEL),
                     conf_ln_g.reshape(1, D_MODEL), conf_ln_b.reshape(1, D_MODEL),
                     w_pw2, conf_b_pw2.reshape(1, D_MODEL), mlp_w1, mlp_w2, final_g.reshape(1, D_MODEL))
```

```python
import functools

import jax
import jax.numpy as jnp
import numpy as np
from jax import lax
from jax.experimental import pallas as pl
from jax.experimental.pallas import tpu as pltpu

F32 = jnp.float32
BF16 = jnp.bfloat16

D_MODEL = 1024
BATCH = 8
SEQ = 2048
CTX_LEN = 256
GRID_W = 64
D_RNN = 1280
N_RNN_BLOCKS = 16
RNN_BLOCK = D_RNN // N_RNN_BLOCKS
REC_CONV_W = 4
RG_C = 8.0
CONF_KW = 31
CONF_HALF = CONF_KW // 2
D_FF = 4 * D_MODEL
N_MOD = 6
EPS = 1e-6
POS_BASE = 10000.0

VMEM_LIMIT_BYTES = 56 * 1024 * 1024
LANES = 128
MXU_N = 256
BD_WIN = 512
BD_K0 = (0, 128, 384, 640, 768)
N_BD = D_RNN // MXU_N

TT = 64
ROWS = TT * BATCH
HALO = 16
HALO_T = HALO // BATCH
XHALO_T = 8
CONF_HALO = 128


def _cparams():
    return pltpu.CompilerParams(dimension_semantics=("arbitrary",), vmem_limit_bytes=VMEM_LIMIT_BYTES)


def _const_spec(shape):
    return pl.BlockSpec(shape, lambda *_: (0,) * len(shape), pipeline_mode=pl.Buffered(1))


def _layer_spec(shape, *idx):
    return pl.BlockSpec((None,) + tuple(shape), lambda *_: tuple(idx), pipeline_mode=pl.Buffered(1))


def _mod_spec(layer, ctx, k):
    return _layer_spec((BATCH, D_MODEL), layer, int(ctx), k)


def _sigmoid(x):
    return 0.5 * jnp.tanh(0.5 * x) + 0.5


def _rms(x3, g):
    ms = jnp.mean(x3 * x3, axis=-1, keepdims=True)
    return x3 * lax.rsqrt(ms + EPS) * g


def _norm_mod(x3, g, sh, sc):
    return _rms(x3, g) * (1.0 + sc) + sh


def _ada_kernel(c_ref, w_ref, b_ref, o_ref):
    c = c_ref[...]
    s = (c * jax.nn.sigmoid(c)).astype(BF16)
    o_ref[...] = jnp.dot(s, w_ref[...].astype(BF16), preferred_element_type=F32) + b_ref[...]


def _ada_table(cc, w_ada, b_ada):
    depth = w_ada.shape[0]
    tn = 1536
    return pl.pallas_call(
        _ada_kernel,
        grid=(depth, N_MOD * D_MODEL // tn),
        in_specs=[
            pl.BlockSpec((2 * BATCH, D_MODEL), lambda l, j: (0, 0)),
            pl.BlockSpec((None, D_MODEL, tn), lambda l, j: (l, 0, j)),
            pl.BlockSpec((None, 1, tn), lambda l, j: (l, 0, j)),
        ],
        out_specs=pl.BlockSpec((None, 2 * BATCH, tn), lambda l, j: (l, 0, j)),
        out_shape=jax.ShapeDtypeStruct((depth, 2 * BATCH, N_MOD * D_MODEL), F32),
        compiler_params=pltpu.CompilerParams(
            dimension_semantics=("arbitrary", "arbitrary"), vmem_limit_bytes=VMEM_LIMIT_BYTES),
        name="ada_table",
    )(cc, w_ada, b_ada.reshape(depth, 1, N_MOD * D_MODEL))


def _fir(src_ref, dst_ref, w_ref, b_ref, *, n_taps, src_off, n_out, width, block):
    for c in range(width // LANES):
        lanes = slice(c * LANES, (c + 1) * LANES)
        wk = [jnp.broadcast_to(w_ref[k:k + 1, lanes], (BATCH, LANES)) for k in range(n_taps)]
        bias = jnp.broadcast_to(b_ref[:, lanes], (BATCH, LANES))
        for base in range(0, n_out * BATCH, block * BATCH):
            acc = [bias] * block
            for q in range(block + n_taps - 1):
                z = src_ref[base + src_off + q * BATCH:base + src_off + (q + 1) * BATCH, lanes]
                for j in range(max(0, q - n_taps + 1), min(block, q + 1)):
                    acc[j] = acc[j] + wk[q - j] * z
            for j in range(block):
                dst_ref[base + j * BATCH:base + (j + 1) * BATCH, lanes] = acc[j]


PACK = 16
FIR_GROUP = 8


def _fir_packed(ze_scr, zo_scr, dst_ref, w_ref, b_ref, *, n_taps, n_out, width, block):
    for c in range(width // LANES):
        lanes = slice(c * LANES, (c + 1) * LANES)
        wk = [jnp.broadcast_to(w_ref[k:k + 1, lanes].astype(BF16), (PACK, LANES)) for k in range(n_taps)]
        bias = jnp.broadcast_to(b_ref[:, lanes], (PACK, LANES))

        def body(blk, carry):
            base = pl.multiple_of(blk * (block * PACK), block * PACK)
            acc = [bias] * block
            part = [None] * block
            cnt = [0] * block
            for tau in range(1, 2 * block + n_taps):
                src, j = (ze_scr, tau // 2) if tau % 2 == 0 else (zo_scr, (tau - 1) // 2)
                zt = src[pl.ds(base + j * PACK, PACK), lanes]
                for p in range(block):
                    k = tau - 2 * p - 1
                    if 0 <= k < n_taps:
                        prod = wk[k] * zt
                        part[p] = prod if part[p] is None else part[p] + prod
                        cnt[p] += 1
                        if cnt[p] % FIR_GROUP == 0 or cnt[p] == n_taps:
                            acc[p] = acc[p] + part[p].astype(F32)
                            part[p] = None
            for p in range(block):
                dst_ref[pl.ds(base + p * PACK, PACK), lanes] = acc[p]
            return carry

        lax.fori_loop(0, (n_out * BATCH // PACK) // block, body, 0, unroll=2)


SCAN_RB = 128
LOG2_E = 1.4426950408889634


def _rglru(u_ref, ub_ref, lam_ref, wab_ref, bab_ref, a_scr, b_scr, h_scr, reverse):
    lam = lam_ref[...]
    softplus = jnp.maximum(-lam, 0.0) + jnp.log1p(jnp.exp(-jnp.abs(lam)))
    k = (-0.5 * RG_C * LOG2_E) * softplus
    n_rb = ROWS // SCAN_RB
    h = h_scr[...]
    for rb in (reversed(range(n_rb)) if reverse else range(n_rb)):
        rows = slice(rb * SCAN_RB, (rb + 1) * SCAN_RB)
        for j in range(N_BD):
            cols = slice(j * MXU_N, (j + 1) * MXU_N)
            pre = jnp.dot(ub_ref[rows, BD_K0[j]:BD_K0[j] + BD_WIN], wab_ref[j],
                          preferred_element_type=F32)
            u = u_ref[rows, cols] if u_ref is not None else ub_ref[rows, cols].astype(F32)
            t_r = jnp.tanh(pre[:, :MXU_N] + bab_ref[0:1, cols])
            t_i = jnp.tanh(pre[:, MXU_N:] + bab_ref[1:2, cols])
            a = jnp.exp2(k[:, cols] * t_r + k[:, cols])
            w = 1.0 - a * a
            sq = jnp.where(w > 0.0, w * lax.rsqrt(w), 0.0)
            a_scr[rows, cols] = a
            b_scr[rows, cols] = (0.5 * sq) * ((t_i + 1.0) * u)
        steps = range(rb * SCAN_RB // BATCH, (rb + 1) * SCAN_RB // BATCH)
        for t in (reversed(steps) if reverse else steps):
            r8 = slice(t * BATCH, (t + 1) * BATCH)
            h = a_scr[r8, :] * h + b_scr[r8, :]
            b_scr[r8, :] = h
    h_scr[...] = h
    return h


NT_WIN = TT + 2 * XHALO_T


def _x_window_copies(x_hbm, xbuf, sem, chunk, slot, kind):
    if kind == "first":
        t0, n, d0 = 0, TT + XHALO_T, XHALO_T
    elif kind == "last":
        t0, n, d0 = chunk * TT - XHALO_T, TT + XHALO_T, 0
    else:
        t0, n, d0 = pl.multiple_of(chunk * TT - XHALO_T, XHALO_T), NT_WIN, 0
    return [pltpu.make_async_copy(x_hbm.at[b, pl.ds(t0, n), :], xbuf.at[slot, pl.ds(d0, n), b, :], sem.at[slot, b])
            for b in range(BATCH)]


def _gelu_tanh(x):
    c = 0.7978845608028654
    inner = x * (c + (c * 0.044715) * (x * x))
    return (0.5 * x) * (1.0 + jnp.tanh(inner))


def _proj_kernel(*refs, nchunks, add_pos, want_gate, n_cast):
    it = iter(refs)
    x_hbm = next(it)
    if add_pos:
        rowtab_ref, coltab_ref = next(it), next(it)
    g_ref, sh_ref, sc_ref = next(it), next(it), next(it)
    wr_ref, cw_ref, cb_ref = next(it), next(it), next(it)
    h0_ref, lam_ref, wab_ref, bab_ref = next(it), next(it), next(it), next(it)
    if want_gate:
        wg_ref = next(it)
    cast_in = [next(it) for _ in range(n_cast)]
    if add_pos:
        x0_ref = next(it)
    u_ref = next(it)
    if want_gate:
        gate_ref = next(it)
        yf_ref = next(it)
    hT_ref = next(it)
    cast_out = [next(it) for _ in range(n_cast)]
    zr_scr, u_scr, a_scr, b_scr, h_scr, xbuf, xsem = (next(it) for _ in range(7))

    i = pl.program_id(0)
    slot = i % 2
    last = nchunks - 1

    @pl.when(i == 0)
    def _():
        h_scr[...] = h0_ref[...]
        xbuf[0, 0:XHALO_T] = jnp.zeros((XHALO_T, BATCH, D_MODEL), F32)
        for cp in _x_window_copies(x_hbm, xbuf, xsem, 0, 0, "first"):
            cp.start()

    @pl.when(i + 1 < last)
    def _():
        for cp in _x_window_copies(x_hbm, xbuf, xsem, i + 1, 1 - slot, "mid"):
            cp.start()

    @pl.when(i + 1 == last)
    def _():
        for cp in _x_window_copies(x_hbm, xbuf, xsem, last, last % 2, "last"):
            cp.start()

    @pl.when(i == 0)
    def _():
        for cp in _x_window_copies(x_hbm, xbuf, xsem, 0, 0, "first"):
            cp.wait()

    @pl.when((i > 0) & (i < last))
    def _():
        for cp in _x_window_copies(x_hbm, xbuf, xsem, i, slot, "mid"):
            cp.wait()

    @pl.when(i == last)
    def _():
        for cp in _x_window_copies(x_hbm, xbuf, xsem, last, last % 2, "last"):
            cp.wait()

    for src, dst in zip(cast_in, cast_out):
        dst[...] = src[...].astype(BF16)

    x3 = xbuf[slot, XHALO_T - HALO_T:XHALO_T + TT + HALO_T]
    if add_pos:
        half = D_MODEL // 2
        r_prev = rowtab_ref[pl.ds(jnp.maximum(i - 1, 0), 1)]
        r_here = rowtab_ref[pl.ds(i, 1)]
        r_next = rowtab_ref[pl.ds(jnp.minimum(i + 1, nchunks - 1), 1)]
        prow = jnp.concatenate([jnp.broadcast_to(r_prev, (HALO_T, 1, half)),
                                jnp.broadcast_to(r_here, (TT, 1, half)),
                                jnp.broadcast_to(r_next, (HALO_T, 1, half))], axis=0)
        pcol = jnp.concatenate([coltab_ref[GRID_W - HALO_T:], coltab_ref[...], coltab_ref[:HALO_T]], axis=0)
        x3 = x3 + jnp.concatenate([prow, pcol], axis=-1)
        x0_ref[...] = x3[HALO_T:HALO_T + TT].reshape(ROWS, D_MODEL)
    h = _norm_mod(x3, g_ref[...], sh_ref[...], sc_ref[...])
    keep_prev = jnp.where(i == 0, 0.0, 1.0)
    keep_next = jnp.where(i == nchunks - 1, 0.0, 1.0)
    h = jnp.concatenate([h[:HALO_T] * keep_prev, h[HALO_T:HALO_T + TT], h[HALO_T + TT:] * keep_next], axis=0)
    h = h.reshape(ROWS + 2 * HALO, D_MODEL).astype(BF16)

    zr_scr[...] = jnp.dot(h, wr_ref[...], preferred_element_type=F32)
    if want_gate:
        zg = jnp.dot(h[HALO:HALO + ROWS], wg_ref[...], preferred_element_type=F32)
        gate_ref[...] = _gelu_tanh(zg).astype(BF16)

    _fir(zr_scr, u_scr, cw_ref, cb_ref, n_taps=REC_CONV_W, src_off=HALO - BATCH, n_out=TT, width=D_RNN, block=16)
    u_ref[...] = u_scr[...].astype(BF16)

    hT_ref[...] = _rglru(u_scr, u_ref, lam_ref, wab_ref, bab_ref, a_scr, b_scr, h_scr, reverse=False)
    if want_gate:
        yf_ref[...] = b_scr[...].astype(BF16)


def _rec_project(x, pos_tabs, mods, ctx, norm_g, w_in, conv_w, conv_b, h0, lam, wab, bab, want_gate, cast=()):
    t_len = x.shape[1]
    n = t_len * BATCH
    nchunks = t_len // TT
    add_pos = pos_tabs is not None
    assert nchunks >= 4

    in_specs = [pl.BlockSpec(memory_space=pl.ANY)]
    args = [x]
    if add_pos:
        assert TT == GRID_W and pos_tabs[0].shape[0] == nchunks
        in_specs += [_const_spec(pos_tabs[0].shape), _const_spec(pos_tabs[1].shape)]
        args += list(pos_tabs)
    in_specs += [
        _layer_spec((1, D_MODEL), 0, 0, 0),
        _mod_spec(0, ctx, 0),
        _mod_spec(0, ctx, 1),
        _layer_spec((D_MODEL, D_RNN), 0, 0, 1),
        _layer_spec((REC_CONV_W, D_RNN), 0, 0, 0),
        _layer_spec((1, D_RNN), 0, 0, 0),
    ]
    args += [norm_g, mods, mods, w_in, conv_w, conv_b]
    in_specs += [
        _const_spec((BATCH, D_RNN)),
        _layer_spec((1, D_RNN), 0, 0, 0),
        _layer_spec((N_BD, BD_WIN, 2 * MXU_N), 0, 0, 0, 0),
        _layer_spec((2, D_RNN), 0, 0, 0),
    ]
    args += [h0, lam, wab, bab]
    if want_gate:
        in_specs.append(_layer_spec((D_MODEL, D_RNN), 0, 0, 0))
        args.append(w_in)
    cast_specs = []
    for w in cast:
        slab = (w.shape[0], w.shape[1] // nchunks, w.shape[2])
        assert w.shape[1] % nchunks == 0 and slab[1] % 16 == 0, w.shape
        cast_specs.append(pl.BlockSpec(slab, lambda i: (0, i, 0)))
    in_specs += cast_specs
    args += list(cast)

    out_specs, out_shape = [], []
    if add_pos:
        out_specs.append(pl.BlockSpec((ROWS, D_MODEL), lambda i: (i, 0)))
        out_shape.append(jax.ShapeDtypeStruct((n, D_MODEL), F32))
    out_specs.append(pl.BlockSpec((ROWS, D_RNN), lambda i: (i, 0)))
    out_shape.append(jax.ShapeDtypeStruct((n, D_RNN), BF16))
    if want_gate:
        out_specs.append(pl.BlockSpec((ROWS, D_RNN), lambda i: (i, 0)))
        out_shape.append(jax.ShapeDtypeStruct((n, D_RNN), BF16))
        out_specs.append(pl.BlockSpec((ROWS, D_RNN), lambda i: (i, 0)))
        out_shape.append(jax.ShapeDtypeStruct((n, D_RNN), BF16))
    out_specs.append(pl.BlockSpec((BATCH, D_RNN), lambda i: (0, 0)))
    out_shape.append(jax.ShapeDtypeStruct((BATCH, D_RNN), F32))
    out_specs += cast_specs
    out_shape += [jax.ShapeDtypeStruct(w.shape, BF16) for w in cast]

    return pl.pallas_call(
        functools.partial(_proj_kernel, nchunks=nchunks, add_pos=add_pos, want_gate=want_gate, n_cast=len(cast)),
        grid=(nchunks,),
        in_specs=in_specs,
        out_specs=out_specs,
        out_shape=out_shape,
        scratch_shapes=[
            pltpu.VMEM((ROWS + 2 * HALO, D_RNN), F32),
            pltpu.VMEM((ROWS, D_RNN), F32),
            pltpu.VMEM((ROWS, D_RNN), F32),
            pltpu.VMEM((ROWS, D_RNN), F32),
            pltpu.VMEM((BATCH, D_RNN), F32),
            pltpu.VMEM((2, NT_WIN, BATCH, D_MODEL), F32),
            pltpu.SemaphoreType.DMA((2, BATCH)),
        ],
        compiler_params=_cparams(),
        name="rec_fwd_lat" if add_pos else "rec_fwd_ctx",
    )(*args)


def _mlp_block(x3, g, sh, sc, gt, w1_ref, w2_ref):
    h = _norm_mod(x3, g, sh, sc).reshape(ROWS, D_MODEL).astype(BF16)
    m = jnp.dot(h, w1_ref[...], preferred_element_type=F32)
    m = jnp.square(jnp.maximum(m, 0.0)).astype(BF16)
    o = jnp.dot(m, w2_ref[...], preferred_element_type=F32)
    return x3 + gt * o.reshape(TT, BATCH, D_MODEL)


def _scan_kernel(*refs, mode):
    it = iter(refs)
    u_ref, h0_ref, lam_ref, wab_ref, bab_ref = next(it), next(it), next(it), next(it), next(it)
    if mode == "out":
        yf_ref, gate_ref, x0_ref, g1_ref, wo_ref = next(it), next(it), next(it), next(it), next(it)
        ng_ref, sh2_ref, sc2_ref, g2_ref, w1_ref, w2_ref = (next(it) for _ in range(6))
        cg_ref, csh_ref, csc_ref, wp_ref, bp_ref = (next(it) for _ in range(5))
        x2_ref, z_ref = next(it), next(it)
    hT_ref = next(it)
    a_scr, b_scr, h_scr = next(it), next(it), next(it)

    i = pl.program_id(0)

    @pl.when(i == 0)
    def _():
        h_scr[...] = h0_ref[...]

    hT_ref[...] = _rglru(None, u_ref, lam_ref, wab_ref, bab_ref, a_scr, b_scr, h_scr, reverse=True)

    if mode == "out":
        y = b_scr[...] + yf_ref[...].astype(F32)
        gy = (gate_ref[...].astype(F32) * y).astype(BF16)
        o = jnp.dot(gy, wo_ref[...], preferred_element_type=F32)
        x1 = x0_ref[...].reshape(TT, BATCH, D_MODEL) + g1_ref[...] * o.reshape(TT, BATCH, D_MODEL)
        x2 = _mlp_block(x1, ng_ref[...], sh2_ref[...], sc2_ref[...], g2_ref[...], w1_ref, w2_ref)
        x2_ref[...] = x2.reshape(ROWS, D_MODEL)
        hc = _norm_mod(x2, cg_ref[...], csh_ref[...], csc_ref[...]).reshape(ROWS, D_MODEL).astype(BF16)
        zz = jnp.dot(hc, wp_ref[...], preferred_element_type=F32) + bp_ref[...]
        z_ref[...] = (zz[:, :D_MODEL] * _sigmoid(zz[:, D_MODEL:])).astype(BF16)


def _rec_scan_rev(u, h0, lam, wab, bab, *, mode, extra=None, mods=None):
    n = u.shape[0]
    nchunks = n // ROWS
    order = lambda i: (nchunks - 1 - i, 0)
    in_specs = [
        pl.BlockSpec((ROWS, D_RNN), order),
        _const_spec((BATCH, D_RNN)),
        _layer_spec((1, D_RNN), 1, 0, 0),
        _layer_spec((N_BD, BD_WIN, 2 * MXU_N), 1, 0, 0, 0),
        _layer_spec((2, D_RNN), 1, 0, 0),
    ]
    args = [u, h0, lam, wab, bab]
    out_specs, out_shape = [], []
    if mode == "out":
        yf, gate, x0, w_out, norm_g, w1, w2, w_pw1, b_pw1 = extra
        in_specs += [
            pl.BlockSpec((ROWS, D_RNN), order),
            pl.BlockSpec((ROWS, D_RNN), order),
            pl.BlockSpec((ROWS, D_MODEL), order),
            _mod_spec(0, False, 2),
            _layer_spec((D_RNN, D_MODEL), 0, 0, 0),
            _layer_spec((1, D_MODEL), 1, 0, 0),
            _mod_spec(0, False, 3),
            _mod_spec(0, False, 4),
            _mod_spec(0, False, 5),
            _layer_spec((D_MODEL, D_FF), 0, 0, 0),
            _layer_spec((D_FF, D_MODEL), 0, 0, 0),
            _layer_spec((1, D_MODEL), 2, 0, 0),
            _mod_spec(1, False, 0),
            _mod_spec(1, False, 1),
            _layer_spec((D_MODEL, 2 * D_MODEL), 0, 0, 0),
            _const_spec((1, 2 * D_MODEL)),
        ]
        args += [yf, gate, x0, mods, w_out, norm_g, mods, mods, mods, w1, w2, norm_g, mods, mods, w_pw1, b_pw1]
        out_specs.append(pl.BlockSpec((ROWS, D_MODEL), order))
        out_shape.append(jax.ShapeDtypeStruct((n, D_MODEL), F32))
        out_specs.append(pl.BlockSpec((ROWS, D_MODEL), order))
        out_shape.append(jax.ShapeDtypeStruct((n, D_MODEL), BF16))
    out_specs.append(pl.BlockSpec((BATCH, D_RNN), lambda i: (0, 0)))
    out_shape.append(jax.ShapeDtypeStruct((BATCH, D_RNN), F32))

    return pl.pallas_call(
        functools.partial(_scan_kernel, mode=mode),
        grid=(nchunks,),
        in_specs=in_specs,
        out_specs=out_specs,
        out_shape=out_shape,
        scratch_shapes=[
            pltpu.VMEM((ROWS, D_RNN), F32),
            pltpu.VMEM((ROWS, D_RNN), F32),
            pltpu.VMEM((BATCH, D_RNN), F32),
        ],
        compiler_params=_cparams(),
        name="rec_rev_" + mode,
    )(*args)


def _conf_out_kernel(zm_ref, zp_ref, zn_ref, x_ref, cw_ref, cb_ref, lg_ref, lb_ref, w_ref, b_ref, g1_ref,
                     ng_ref, sh2_ref, sc2_ref, g2_ref, w1_ref, w2_ref, fg_ref,
                     o_hbm, zext_scr, conv_scr, ze_scr, zo_scr, ybuf, osem, *, nchunks):
    i = pl.program_id(0)
    slot = i % 2

    def out_copies(chunk, slot_):
        return [pltpu.make_async_copy(ybuf.at[slot_, :, b, :], o_hbm.at[b, pl.ds(chunk * TT, TT), :], osem.at[slot_, b])
                for b in range(BATCH)]

    @pl.when(i >= 2)
    def _():
        for cp in out_copies(i - 2, slot):
            cp.wait()

    zext_scr[0:CONF_HALO, :] = zp_ref[...].astype(F32)
    zext_scr[CONF_HALO:CONF_HALO + ROWS, :] = zm_ref[...].astype(F32)
    zext_scr[CONF_HALO + ROWS:, :] = zn_ref[...].astype(F32)

    @pl.when(i == 0)
    def _():
        zext_scr[0:CONF_HALO, :] = jnp.zeros((CONF_HALO, D_MODEL), F32)

    @pl.when(i == nchunks - 1)
    def _():
        zext_scr[CONF_HALO + ROWS:, :] = jnp.zeros((CONF_HALO, D_MODEL), F32)

    assert CONF_HALO // BATCH - CONF_HALF == 1
    n_even = ROWS + 2 * CONF_HALO
    ze_scr[...] = zext_scr[0:n_even, :].astype(BF16)
    zo_scr[...] = zext_scr[BATCH:BATCH + n_even - PACK, :].astype(BF16)
    _fir_packed(ze_scr, zo_scr, conv_scr, cw_ref, cb_ref, n_taps=CONF_KW, n_out=TT, width=D_MODEL, block=8)

    z = conv_scr[...]
    mu = jnp.mean(z, axis=-1, keepdims=True)
    zc = z - mu
    var = jnp.mean(zc * zc, axis=-1, keepdims=True)
    zl = zc * lax.rsqrt(var + EPS) * lg_ref[...] + lb_ref[...]
    zs = (zl * _sigmoid(zl)).astype(BF16)
    y = jnp.dot(zs, w_ref[...], preferred_element_type=F32) + b_ref[...]
    x3 = x_ref[...].reshape(TT, BATCH, D_MODEL) + g1_ref[...] * y.reshape(TT, BATCH, D_MODEL)
    x4 = _mlp_block(x3, ng_ref[...], sh2_ref[...], sc2_ref[...], g2_ref[...], w1_ref, w2_ref)
    ybuf[slot] = _rms(x4, fg_ref[...])
    for cp in out_copies(i, slot):
        cp.start()

    @pl.when(i == nchunks - 1)
    def _():
        for cp in out_copies(i - 1, 1 - slot) + out_copies(i, slot):
            cp.wait()


def _conf_out(z, x, mods, layer, norm_g, conv_w, conv_b, ln_g, ln_b, w_pw2, b_pw2, w1, w2, final_g):
    n = x.shape[0]
    nchunks = n // ROWS
    hb = ROWS // CONF_HALO
    return pl.pallas_call(
        functools.partial(_conf_out_kernel, nchunks=nchunks),
        grid=(nchunks,),
        in_specs=[
            pl.BlockSpec((ROWS, D_MODEL), lambda i: (i, 0)),
            pl.BlockSpec((CONF_HALO, D_MODEL), lambda i: (jnp.maximum(i * hb - 1, 0), 0)),
            pl.BlockSpec((CONF_HALO, D_MODEL), lambda i: (jnp.minimum((i + 1) * hb, n // CONF_HALO - 1), 0)),
            pl.BlockSpec((ROWS, D_MODEL), lambda i: (i, 0)),
            _layer_spec((CONF_KW, D_MODEL), 0, 0, 0),
            _const_spec((1, D_MODEL)),
            _const_spec((1, D_MODEL)),
            _const_spec((1, D_MODEL)),
            _layer_spec((D_MODEL, D_MODEL), 0, 0, 0),
            _const_spec((1, D_MODEL)),
            _mod_spec(layer, False, 2),
            _layer_spec((1, D_MODEL), 2 * layer + 1, 0, 0),
            _mod_spec(layer, False, 3),
            _mod_spec(layer, False, 4),
            _mod_spec(layer, False, 5),
            _layer_spec((D_MODEL, D_FF), layer, 0, 0),
            _layer_spec((D_FF, D_MODEL), layer, 0, 0),
            _const_spec((1, D_MODEL)),
        ],
        out_specs=pl.BlockSpec(memory_space=pl.ANY),
        out_shape=jax.ShapeDtypeStruct((BATCH, n // BATCH, D_MODEL), F32),
        scratch_shapes=[
            pltpu.VMEM((ROWS + 2 * CONF_HALO, D_MODEL), F32),
            pltpu.VMEM((ROWS, D_MODEL), F32),
            pltpu.VMEM((ROWS + 2 * CONF_HALO, D_MODEL), BF16),
            pltpu.VMEM((ROWS + 2 * CONF_HALO - PACK, D_MODEL), BF16),
            pltpu.VMEM((2, TT, BATCH, D_MODEL), F32),
            pltpu.SemaphoreType.DMA((2, BATCH)),
        ],
        compiler_params=_cparams(),
        name="conf_out",
    )(z, z, z, x, conv_w, conv_b, ln_g, ln_b, w_pw2, b_pw2, mods, norm_g, mods, mods, mods, w1, w2, final_g)


def _pos_tables(rows, d):
    q = d // 4
    omega = 1.0 / (POS_BASE ** (jnp.arange(q, dtype=F32) / q))
    er = jnp.arange(rows, dtype=jnp.int32).astype(F32)[:, None] * omega[None, :]
    ec = jnp.arange(GRID_W, dtype=jnp.int32).astype(F32)[:, None] * omega[None, :]
    rowtab = jnp.concatenate([jnp.sin(er), jnp.cos(er)], axis=-1)
    coltab = jnp.concatenate([jnp.sin(ec), jnp.cos(ec)], axis=-1)
    return rowtab.reshape(rows, 1, d // 2), coltab.reshape(GRID_W, 1, d // 2)


def _block_diag_windows(w_a, w_x):
    w = jnp.stack([w_a, w_x], axis=1).reshape(2, 2, D_RNN, RNN_BLOCK)
    tiled_eye = np.tile(np.eye(RNN_BLOCK, dtype=np.float32), (1, N_RNN_BLOCKS))
    row_blk = np.arange(D_RNN)[:, None] // RNN_BLOCK
    col_blk = np.arange(D_RNN)[None, :] // RNN_BLOCK
    w_win, sel, msk = [], [], []
    for j, k0 in enumerate(BD_K0):
        cols = slice(j * MXU_N, (j + 1) * MXU_N)
        w_win.append(w[:, :, k0:k0 + BD_WIN])
        sel.append(tiled_eye[:, cols])
        msk.append(0.5 * (row_blk[k0:k0 + BD_WIN] == col_blk[:, cols]))
    w_win = jnp.stack(w_win, axis=2)
    sel = jnp.asarray(np.stack(sel))
    msk = jnp.asarray(np.stack(msk), dtype=F32)
    full = jnp.einsum("dgjrk,jkc->djrgc", w_win, sel) * msk[None, :, :, None, :]
    return full.reshape(2, N_BD, BD_WIN, 2 * MXU_N).astype(BF16)


def kernel(x, c, ctx, c_ctx, w_ada, b_ada, norm_g, rec_w_in, rec_conv_w, rec_conv_b, rec_lambda, rec_w_a, rec_b_a,
           rec_w_x, rec_b_x, rec_w_out, conf_w_pw1, conf_b_pw1, conf_conv_w, conf_conv_b, conf_ln_g, conf_ln_b,
           conf_w_pw2, conf_b_pw2, mlp_w_in, mlp_w_out, final_g):
    assert x.shape == (BATCH, SEQ, D_MODEL) and ctx.shape == (BATCH, CTX_LEN, D_MODEL)
    assert w_ada.shape[0] == 2, "one recurrent layer followed by one conformer layer"

    cc = jnp.concatenate([c, jnp.broadcast_to(c_ctx[None, :], (BATCH, D_MODEL))], axis=0)
    mods = _ada_table(cc, w_ada, b_ada)
    norm_g = norm_g.reshape(4, 1, D_MODEL)

    w_in = rec_w_in.astype(BF16)
    conv_b = rec_conv_b.reshape(1, 1, D_RNN)
    wab = _block_diag_windows(rec_w_a[0], rec_w_x[0])
    bab = 0.5 * jnp.stack([rec_b_a[0].reshape(2, D_RNN), rec_b_x[0].reshape(2, D_RNN)], axis=1)
    lam = rec_lambda[0].reshape(2, 1, D_RNN)
    zeros = jnp.zeros((BATCH, D_RNN), F32)
    rec = (lam, wab, bab)

    u_ctx, h0_f = _rec_project(ctx, None, mods, True, norm_g, w_in, rec_conv_w, conv_b, zeros, *rec, False)
    (h0_r,) = _rec_scan_rev(u_ctx, zeros, *rec, mode="state")

    pos_tabs = _pos_tables(SEQ // GRID_W, D_MODEL)
    x0, u_lat, gate, y_f, _, mlp_w1, mlp_w2, w_pw1, w_pw2 = _rec_project(
        x, pos_tabs, mods, False, norm_g, w_in, rec_conv_w, conv_b, h0_f, *rec, True,
        cast=(mlp_w_in, mlp_w_out, conf_w_pw1, conf_w_pw2))
    x2, z, _ = _rec_scan_rev(u_lat, h0_r, *rec, mode="out", mods=mods,
                             extra=(y_f, gate, x0, rec_w_out.astype(BF16), norm_g, mlp_w1, mlp_w2,
                                    w_pw1, conf_b_pw1.reshape(1, 2 * D_MODEL)))

    return _conf_out(z, x2, mods, 1, norm_g, conf_conv_w, conf_conv_b.reshape(1, D_MODEL),
                     conf_ln_g.reshape(1, D_MODEL), conf_ln_b.reshape(1, D_MODEL),
                     w_pw2, conf_b_pw2.reshape(1, D_MODEL), mlp_w1, mlp_w2, final_g.reshape(1, D_MODEL))
```
